```python
import jax
import jax.numpy as jnp
from jax import lax
import numpy as np

D_MODEL = 4096
BATCH = 2
SEQ = 8192
DEPTH = 2
DEC_BATCH = 32
DEC_SEQ = 32
PAST_LEN = 2048

CHUNK = 64
HEAD_DIM = 128
N_EVEN = (DEPTH + 1) // 2
N_ODD = DEPTH // 2
D_FF = 11008
EPS = 1e-6
NORM_FFN1_PRE = 0
NORM_FFN1_POST = 1
NORM_MIX_PRE = 2
NORM_MIX_POST = 3
NORM_X_PRE = 4
NORM_X_POST = 5
NORM_FFN2_PRE = 6
NORM_FFN2_POST = 7
NORM_MEM = 8
N_NORMS = 9
W_A = D_MODEL // 2
CONV_A = 3
W_B = D_MODEL // 2
CONV_B = 31
N_Q_HEADS = D_MODEL // (2 * HEAD_DIM)
N_KV_HEADS = 4
GQA_GROUP = N_Q_HEADS // N_KV_HEADS
WINDOW = 128
ROPE_THETA = 10000.0
W_D = D_MODEL // 2
GMLP_CHUNK = 128
GMLP_GROUPS = 8
GMLP_GROUP_DIM = W_D // GMLP_GROUPS
N_MEM = 256
N_X_HEADS = 4
W_X = N_X_HEADS * HEAD_DIM
Q_W = N_Q_HEADS * HEAD_DIM
KV_W = N_KV_HEADS * HEAD_DIM
IN_AB = 3 * W_A + 2 * W_B
OUT_AB = W_A + W_B
IN_CD = Q_W + 2 * KV_W + 2 * W_D
OUT_CD = Q_W + W_D

kernel_name = 'hybrid_streaming_encoder_step'


def rms_norm(x, g):
    xf = x.astype(jnp.float32)
    y = xf * lax.rsqrt(jnp.mean(xf * xf, axis=-1, keepdims=True) + EPS)
    return (y * g.astype(jnp.float32)).astype(x.dtype)


def layer_norm(x, g, b):
    xf = x.astype(jnp.float32)
    mu = jnp.mean(xf, axis=-1, keepdims=True)
    var = jnp.mean(jnp.square(xf - mu), axis=-1, keepdims=True)
    y = (xf - mu) * lax.rsqrt(var + EPS)
    return (y * g.astype(jnp.float32) + b.astype(jnp.float32)).astype(x.dtype)


def half_ffn(x, g_pre, g_post, w_gate, w_up, w_down):
    h = rms_norm(x, g_pre)
    f = (jax.nn.silu(h @ w_gate) * (h @ w_up)) @ w_down
    return x + 0.5 * rms_norm(f, g_post)


def causal_dwconv(x_ext, w):
    c = w.shape[1]
    return lax.conv_general_dilated(x_ext, w[:, None, :], window_strides=(1,), padding='VALID',
                                    dimension_numbers=('NWC', 'WIO', 'NWC'), feature_group_count=c)


def rope(x, pos):
    half = HEAD_DIM // 2
    inv_freq = ROPE_THETA ** (-jnp.arange(half, dtype=jnp.float32) / half)
    ang = pos.astype(jnp.float32)[:, None] * inv_freq[None, :]
    cos = jnp.cos(ang)[:, None, :]
    sin = jnp.sin(ang)[:, None, :]
    xf = x.astype(jnp.float32)
    x1, x2 = xf[..., :half], xf[..., half:]
    return jnp.concatenate([x1 * cos - x2 * sin, x2 * cos + x1 * sin], axis=-1).astype(x.dtype)


def sink_attention(q, k, v, sink, mask):
    s = jnp.einsum('...qhgd,...khd->...hgqk', q, k).astype(jnp.float32) * (HEAD_DIM ** -0.5)
    if mask is not None:
        s = jnp.where(mask, s, -jnp.inf)
    sk = sink.astype(jnp.float32)[:, :, None, None]
    m = jnp.maximum(jnp.max(s, axis=-1, keepdims=True), sk)
    e = jnp.exp(s - m)
    p = e / (jnp.sum(e, axis=-1, keepdims=True) + jnp.exp(sk - m))
    return jnp.einsum('...hgqk,...khd->...qhgd', p.astype(v.dtype), v)


def band_rows(k):
    b, t, h, d = k.shape
    nc = t // CHUNK
    n_back = WINDOW // CHUNK
    kp = jnp.pad(k, ((0, 0), (WINDOW, 0), (0, 0), (0, 0))).reshape(b, nc + n_back, CHUNK, h, d)
    return jnp.concatenate([kp[:, i:i + nc] for i in range(n_back + 1)], axis=2)


def gmlp_weights(w_s):
    i = jnp.arange(GMLP_CHUNK)
    allowed = (i[None, :] // CHUNK) <= (i[:, None] // CHUNK)
    return jnp.where(allowed[None], w_s, jnp.zeros((), w_s.dtype))


def mixer_ab(h, hist_a, hist_b, w_in, a_conv_w, b_conv_w, b_conv_bias, b_ln_g, b_ln_b, w_out):
    proj = h @ w_in
    gate_out, gate_in, val, glu_a, glu_g = jnp.split(proj, [W_A, 2 * W_A, 3 * W_A, 3 * W_A + W_B], axis=-1)
    ext_a = jnp.concatenate([hist_a, gate_in * val], axis=1)
    y_a = gate_out * causal_dwconv(ext_a, a_conv_w)
    ext_b = jnp.concatenate([hist_b, glu_a * jax.nn.sigmoid(glu_g)], axis=1)
    z = causal_dwconv(ext_b, b_conv_w) + b_conv_bias
    y_b = jax.nn.silu(layer_norm(z, b_ln_g, b_ln_b))
    y = jnp.concatenate([y_a, y_b], axis=-1) @ w_out
    return y, ext_a[:, -(CONV_A - 1):], ext_b[:, -(CONV_B - 1):]


def project_cd(h, pos, w_in, d_ln_g, d_ln_b):
    b, t, _ = h.shape
    proj = h @ w_in
    q, k, v, u, vg = jnp.split(proj, [Q_W, Q_W + KV_W, Q_W + 2 * KV_W, Q_W + 2 * KV_W + W_D], axis=-1)
    q = rope(q.reshape(b, t, N_Q_HEADS, HEAD_DIM), pos).reshape(b, t, N_KV_HEADS, GQA_GROUP, HEAD_DIM)
    k = rope(k.reshape(b, t, N_KV_HEADS, HEAD_DIM), pos)
    v = v.reshape(b, t, N_KV_HEADS, HEAD_DIM)
    vn = layer_norm(vg, d_ln_g, d_ln_b)
    return q, k, v, u, vn


def mixer_cd_prompt(h, pos, w_in, sink, d_ln_g, d_ln_b, d_w_s, d_b_s, w_out):
    b, t, _ = h.shape
    q, k, v, u, vn = project_cd(h, pos, w_in, d_ln_g, d_ln_b)
    nc = t // CHUNK
    qc = q.reshape(b, nc, CHUNK, N_KV_HEADS, GQA_GROUP, HEAD_DIM)
    key_pos = jnp.arange(nc)[:, None] * CHUNK - WINDOW + jnp.arange(WINDOW + CHUNK)[None, :]
    mask = (key_pos >= 0)[None, :, None, None, None, :]
    att = sink_attention(qc, band_rows(k), band_rows(v), sink, mask).reshape(b, t, Q_W)
    nd = t // GMLP_CHUNK
    vc = vn.reshape(b, nd, GMLP_CHUNK, GMLP_GROUPS, GMLP_GROUP_DIM)
    sg = jnp.einsum('gij,bnjgc->bnigc', gmlp_weights(d_w_s), vc) + d_b_s.T[None, None, :, :, None]
    y_d = u * sg.reshape(b, t, W_D)
    y = jnp.concatenate([att, y_d], axis=-1) @ w_out
    return y, k[:, -WINDOW:], v[:, -WINDOW:]


def mixer_cd_sample(h, pos, win_k, win_v, w_in, sink, d_ln_g, d_ln_b, d_w_s, d_b_s, w_out):
    b, s, _ = h.shape
    q, k, v, u, vn = project_cd(h, pos, w_in, d_ln_g, d_ln_b)
    k_all = jnp.concatenate([win_k, k], axis=1)
    v_all = jnp.concatenate([win_v, v], axis=1)
    att = sink_attention(q, k_all, v_all, sink, None).reshape(b, s, Q_W)
    w_s = gmlp_weights(d_w_s)[:, :s, :s]
    sg = jnp.einsum('gij,bjgc->bigc', w_s, vn.reshape(b, s, GMLP_GROUPS, GMLP_GROUP_DIM)) + d_b_s[:, :s].T[None, :, :, None]
    y_d = u * sg.reshape(b, s, W_D)
    y = jnp.concatenate([att, y_d], axis=-1) @ w_out
    return y, k_all[:, -WINDOW:], v_all[:, -WINDOW:], vn


def mem_kv(mem, g, wk, wv):
    b, n, _ = mem.shape
    m = rms_norm(mem, g)
    return (m @ wk).reshape(b, n, N_X_HEADS, HEAD_DIM), (m @ wv).reshape(b, n, N_X_HEADS, HEAD_DIM)


def cross_attn(h, mk, mv, wq, wo):
    b, t, _ = h.shape
    q = (h @ wq).reshape(b, t, N_X_HEADS, HEAD_DIM)
    s = jnp.einsum('bthd,bnhd->bhtn', q, mk).astype(jnp.float32) * (HEAD_DIM ** -0.5)
    p = jax.nn.softmax(s, axis=-1).astype(mv.dtype)
    o = jnp.einsum('bhtn,bnhd->bthd', p, mv).reshape(b, t, W_X)
    return o @ wo


def setup_inputs(seed: int = 0) -> dict:
    key = jax.random.key(seed)
    ks = jax.random.split(key, 32)
    f32 = jnp.float32

    def rnd(k, shape, scale):
        return jax.random.normal(k, shape, f32) * scale

    return {
        'x_prompt': rnd(ks[0], (BATCH, SEQ, D_MODEL), 1.0),
        'x_sample': rnd(ks[1], (DEC_BATCH, DEC_SEQ, D_MODEL), 1.0),
        'mem_prompt': rnd(ks[2], (BATCH, N_MEM, D_MODEL), 1.0),
        'state_conv_a': rnd(ks[3], (N_EVEN, DEC_BATCH, CONV_A - 1, W_A), 1.0),
        'state_conv_b': rnd(ks[4], (N_EVEN, DEC_BATCH, CONV_B - 1, W_B), 1.0),
        'cache_win_k': rnd(ks[5], (N_ODD, DEC_BATCH, WINDOW, N_KV_HEADS, HEAD_DIM), 1.0),
        'cache_win_v': rnd(ks[6], (N_ODD, DEC_BATCH, WINDOW, N_KV_HEADS, HEAD_DIM), 1.0),
        'cache_mem_k': rnd(ks[7], (DEPTH, DEC_BATCH, N_MEM, N_X_HEADS, HEAD_DIM), 1.0),
        'cache_mem_v': rnd(ks[8], (DEPTH, DEC_BATCH, N_MEM, N_X_HEADS, HEAD_DIM), 1.0),
        'norms': 1.0 + rnd(ks[9], (DEPTH, N_NORMS, D_MODEL), 0.02),
        'ffn_w_gate': rnd(ks[10], (DEPTH, 2, D_MODEL, D_FF), D_MODEL ** -0.5),
        'ffn_w_up': rnd(ks[11], (DEPTH, 2, D_MODEL, D_FF), D_MODEL ** -0.5),
        'ffn_w_down': rnd(ks[12], (DEPTH, 2, D_FF, D_MODEL), D_FF ** -0.5),
        'xattn_wq': rnd(ks[13], (DEPTH, D_MODEL, W_X), D_MODEL ** -0.5),
        'xattn_wk': rnd(ks[14], (DEPTH, D_MODEL, W_X), D_MODEL ** -0.5),
        'xattn_wv': rnd(ks[15], (DEPTH, D_MODEL, W_X), D_MODEL ** -0.5),
        'xattn_wo': rnd(ks[16], (DEPTH, W_X, D_MODEL), W_X ** -0.5),
        'ab_w_in': rnd(ks[17], (N_EVEN, D_MODEL, IN_AB), D_MODEL ** -0.5),
        'a_conv_w': rnd(ks[18], (N_EVEN, CONV_A, W_A), CONV_A ** -0.5),
        'b_conv_w': rnd(ks[19], (N_EVEN, CONV_B, W_B), CONV_B ** -0.5),
        'b_conv_bias': rnd(ks[20], (N_EVEN, W_B), 0.02),
        'b_ln_g': 1.0 + rnd(ks[21], (N_EVEN, W_B), 0.02),
        'b_ln_b': rnd(ks[22], (N_EVEN, W_B), 0.02),
        'ab_w_out': rnd(ks[23], (N_EVEN, OUT_AB, D_MODEL), OUT_AB ** -0.5),
        'cd_w_in': rnd(ks[24], (N_ODD, D_MODEL, IN_CD), D_MODEL ** -0.5),
        'c_sink': rnd(ks[25], (N_ODD, N_KV_HEADS, GQA_GROUP), 0.5),
        'd_ln_g': 1.0 + rnd(ks[26], (N_ODD, W_D), 0.02),
        'd_ln_b': rnd(ks[27], (N_ODD, W_D), 0.02),
        'd_w_s': rnd(ks[28], (N_ODD, GMLP_GROUPS, GMLP_CHUNK, GMLP_CHUNK), GMLP_CHUNK ** -0.5),
        'd_b_s': 1.0 + rnd(ks[29], (N_ODD, GMLP_GROUPS, GMLP_CHUNK), 0.02),
        'cd_w_out': rnd(ks[30], (N_ODD, OUT_CD, D_MODEL), OUT_CD ** -0.5),
    }


def reference(x_prompt, x_sample, mem_prompt, state_conv_a, state_conv_b, cache_win_k, cache_win_v,
              cache_mem_k, cache_mem_v, norms, ffn_w_gate, ffn_w_up, ffn_w_down,
              xattn_wq, xattn_wk, xattn_wv, xattn_wo,
              ab_w_in, a_conv_w, b_conv_w, b_conv_bias, b_ln_g, b_ln_b, ab_w_out,
              cd_w_in, c_sink, d_ln_g, d_ln_b, d_w_s, d_b_s, cd_w_out):
    pos_prompt = jnp.arange(x_prompt.shape[1], dtype=jnp.int32)
    pos_sample = PAST_LEN + jnp.arange(x_sample.shape[1], dtype=jnp.int32)
    xp, xs = x_prompt, x_sample
    conv_a_p, conv_a_s, conv_b_p, conv_b_s = [], [], [], []
    win_k_p, win_v_p, win_k_s, win_v_s, chunk_v_s = [], [], [], [], []
    mem_k_p, mem_v_p = [], []
    for layer in range(DEPTH):
        g = norms[layer]
        xp = half_ffn(xp, g[NORM_FFN1_PRE], g[NORM_FFN1_POST], ffn_w_gate[layer, 0], ffn_w_up[layer, 0], ffn_w_down[layer, 0])
        xs = half_ffn(xs, g[NORM_FFN1_PRE], g[NORM_FFN1_POST], ffn_w_gate[layer, 0], ffn_w_up[layer, 0], ffn_w_down[layer, 0])
        hp = rms_norm(xp, g[NORM_MIX_PRE])
        hs = rms_norm(xs, g[NORM_MIX_PRE])
        if layer % 2 == 0:
            e = layer // 2
            ab = (ab_w_in[e], a_conv_w[e], b_conv_w[e], b_conv_bias[e], b_ln_g[e], b_ln_b[e], ab_w_out[e])
            zero_a = jnp.zeros((xp.shape[0], CONV_A - 1, W_A), xp.dtype)
            zero_b = jnp.zeros((xp.shape[0], CONV_B - 1, W_B), xp.dtype)
            yp, ca_p, cb_p = mixer_ab(hp, zero_a, zero_b, *ab)
            ys, ca_s, cb_s = mixer_ab(hs, state_conv_a[e], state_conv_b[e], *ab)
            conv_a_p.append(ca_p)
            conv_a_s.append(ca_s)
            conv_b_p.append(cb_p)
            conv_b_s.append(cb_s)
        else:
            o = layer // 2
            cd = (cd_w_in[o], c_sink[o], d_ln_g[o], d_ln_b[o], d_w_s[o], d_b_s[o], cd_w_out[o])
            yp, wk_p, wv_p = mixer_cd_prompt(hp, pos_prompt, *cd)
            ys, wk_s, wv_s, vrow_s = mixer_cd_sample(hs, pos_sample, cache_win_k[o], cache_win_v[o], *cd)
            win_k_p.append(wk_p)
            win_v_p.append(wv_p)
            win_k_s.append(wk_s)
            win_v_s.append(wv_s)
            chunk_v_s.append(vrow_s)
        xp = xp + rms_norm(yp, g[NORM_MIX_POST])
        xs = xs + rms_norm(ys, g[NORM_MIX_POST])
        mk_p, mv_p = mem_kv(mem_prompt, g[NORM_MEM], xattn_wk[layer], xattn_wv[layer])
        mem_k_p.append(mk_p)
        mem_v_p.append(mv_p)
        xp = xp + rms_norm(cross_attn(rms_norm(xp, g[NORM_X_PRE]), mk_p, mv_p, xattn_wq[layer], xattn_wo[layer]), g[NORM_X_POST])
        xs = xs + rms_norm(cross_attn(rms_norm(xs, g[NORM_X_PRE]), cache_mem_k[layer], cache_mem_v[layer], xattn_wq[layer], xattn_wo[layer]), g[NORM_X_POST])
        xp = half_ffn(xp, g[NORM_FFN2_PRE], g[NORM_FFN2_POST], ffn_w_gate[layer, 1], ffn_w_up[layer, 1], ffn_w_down[layer, 1])
        xs = half_ffn(xs, g[NORM_FFN2_PRE], g[NORM_FFN2_POST], ffn_w_gate[layer, 1], ffn_w_up[layer, 1], ffn_w_down[layer, 1])
    y_prompt = xp
    y_sample = xs
    new_conv_a_prompt = jnp.stack(conv_a_p)
    new_conv_a_sample = jnp.stack(conv_a_s)
    new_conv_b_prompt = jnp.stack(conv_b_p)
    new_conv_b_sample = jnp.stack(conv_b_s)
    new_win_k_prompt = jnp.stack(win_k_p)
    new_win_v_prompt = jnp.stack(win_v_p)
    new_win_k_sample = jnp.stack(win_k_s)
    new_win_v_sample = jnp.stack(win_v_s)
    new_chunk_v_sample = jnp.stack(chunk_v_s)
    new_mem_k_prompt = jnp.stack(mem_k_p)
    new_mem_v_prompt = jnp.stack(mem_v_p)
    return (y_prompt, y_sample, new_conv_a_prompt, new_conv_a_sample, new_conv_b_prompt, new_conv_b_sample,
            new_win_k_prompt, new_win_v_prompt, new_win_k_sample, new_win_v_sample, new_chunk_v_sample,
            new_mem_k_prompt, new_mem_v_prompt)
```

```python
import functools

import jax
import jax.numpy as jnp
from jax import lax
from jax.experimental import pallas as pl
from jax.experimental.pallas import tpu as pltpu

EPS = 1e-6
CHUNK = 64
HEAD_DIM = 128
WINDOW = 128
ROPE_THETA = 10000.0
GMLP_CHUNK = 128
GMLP_GROUPS = 8
N_KV_HEADS = 4
GQA_GROUP = 4
N_X_HEADS = 4
CONV_A = 3
CONV_B = 31
PAST_LEN = 2048
N_NORMS = 9

V7X_VMEM_BYTES = 64 * 1024 * 1024
VMEM_LIMIT = V7X_VMEM_BYTES - 8 * 1024 * 1024
SUBLANE = 8
LANE = 128
HALO_A = SUBLANE
HALO_B = 4 * SUBLANE
FFN_PAD = 1024

BF16 = jnp.bfloat16
F32 = jnp.float32


def _tile(n, target, mult):
    t = min(n, target)
    t -= t % mult
    while t >= mult:
        if n % t == 0:
            return t
        t -= mult
    return n


def _params(*sem):
    return pltpu.CompilerParams(dimension_semantics=sem, vmem_limit_bytes=VMEM_LIMIT)


def _rms(x, g):
    return x * lax.rsqrt(jnp.mean(x * x, axis=-1, keepdims=True) + EPS) * g


def _layer_norm(x, g, b):
    mu = jnp.mean(x, axis=-1, keepdims=True)
    xc = x - mu
    var = jnp.mean(xc * xc, axis=-1, keepdims=True)
    return xc * lax.rsqrt(var + EPS) * g + b


def _silu(x):
    return x * jax.nn.sigmoid(x)


def _rms_cast_kernel(x_ref, g_ref, o_ref):
    o_ref[...] = _rms(x_ref[...], g_ref[...]).astype(o_ref.dtype)


def rms_cast(x, g):
    m, d = x.shape
    tm = _tile(m, 256, SUBLANE)
    return pl.pallas_call(
        _rms_cast_kernel,
        out_shape=jax.ShapeDtypeStruct((m, d), BF16),
        grid=(m // tm,),
        in_specs=[pl.BlockSpec((tm, d), lambda i: (i, 0)),
                  pl.BlockSpec((1, d), lambda i: (0, 0))],
        out_specs=pl.BlockSpec((tm, d), lambda i: (i, 0)),
        compiler_params=_params("parallel"),
        name="rms_cast",
    )(x, g.reshape(1, d))


def _mm_kernel(a_ref, w_ref, o_ref):
    o_ref[...] = jnp.dot(a_ref[...], w_ref[...], preferred_element_type=F32).astype(o_ref.dtype)


def matmul_cols(a, w, col0, ncols, out_dtype=F32):
    m, k = a.shape
    tm = _tile(m, 1024, SUBLANE)
    tn = _tile(ncols, 1024, LANE)
    assert col0 % tn == 0
    cb = col0 // tn
    return pl.pallas_call(
        _mm_kernel,
        out_shape=jax.ShapeDtypeStruct((m, ncols), out_dtype),
        grid=(m // tm, ncols // tn),
        in_specs=[pl.BlockSpec((tm, k), lambda i, j: (i, 0)),
                  pl.BlockSpec((k, tn), lambda i, j: (0, cb + j))],
        out_specs=pl.BlockSpec((tm, tn), lambda i, j: (i, j)),
        compiler_params=_params("parallel", "parallel"),
        name="matmul_cols",
    )(a, w)


def _swiglu_kernel(h_ref, wg_ref, wu_ref, o_ref):
    h = h_ref[...]
    g = jnp.dot(h, wg_ref[...], preferred_element_type=F32)
    u = jnp.dot(h, wu_ref[...], preferred_element_type=F32)
    o_ref[...] = (_silu(g) * u).astype(o_ref.dtype)


def swiglu_up(h, wg, wu):
    m, k = h.shape
    f = wg.shape[1]
    tm = _tile(m, 1024, SUBLANE)
    tn = _tile(f, 512, LANE)
    return pl.pallas_call(
        _swiglu_kernel,
        out_shape=jax.ShapeDtypeStruct((m, f), BF16),
        grid=(m // tm, f // tn),
        in_specs=[pl.BlockSpec((tm, k), lambda i, j: (i, 0)),
                  pl.BlockSpec((k, tn), lambda i, j: (0, j)),
                  pl.BlockSpec((k, tn), lambda i, j: (0, j))],
        out_specs=pl.BlockSpec((tm, tn), lambda i, j: (i, j)),
        compiler_params=_params("parallel", "parallel"),
        name="swiglu_up",
    )(h, wg, wu)


def _mm_resid_kernel(a_ref, w_ref, x_ref, gpost_ref, gnext_ref, xo_ref, *rest, scale, nk, n_chunk, with_next):
    k = pl.program_id(1)
    n = xo_ref.shape[1]
    a = a_ref[...]

    @pl.when(k == 0)
    def _():
        for c in range(0, n, n_chunk):
            xo_ref[:, c:c + n_chunk] = jnp.dot(a, w_ref[:, c:c + n_chunk], preferred_element_type=F32)

    @pl.when(k > 0)
    def _():
        for c in range(0, n, n_chunk):
            xo_ref[:, c:c + n_chunk] += jnp.dot(a, w_ref[:, c:c + n_chunk], preferred_element_type=F32)

    @pl.when(k == nk - 1)
    def _():
        tm = xo_ref.shape[0]
        rc = min(tm, 128)
        for r0 in range(0, tm, rc):
            rows = slice(r0, r0 + rc)
            y = _rms(xo_ref[rows, :], gpost_ref[...])
            if scale != 1.0:
                y = scale * y
            xn = x_ref[rows, :] + y
            xo_ref[rows, :] = xn
            if with_next:
                rest[0][rows, :] = _rms(xn, gnext_ref[...]).astype(BF16)


def matmul_resid_norm(a, w, x, g_post, g_next, scale):
    m, kdim = a.shape
    n = w.shape[1]
    tm = _tile(m, 512, SUBLANE)
    tk = _tile(kdim, 1024, LANE)
    nk = kdim // tk
    with_next = g_next is not None
    gn = g_next if with_next else g_post
    out_shape = [jax.ShapeDtypeStruct((m, n), F32)]
    out_specs = [pl.BlockSpec((tm, n), lambda i, k: (i, 0))]
    if with_next:
        out_shape.append(jax.ShapeDtypeStruct((m, n), BF16))
        out_specs.append(pl.BlockSpec((tm, n), lambda i, k: (i, 0)))
    kern = functools.partial(_mm_resid_kernel, scale=scale, nk=nk, n_chunk=_tile(n, 1024, LANE),
                             with_next=with_next)
    outs = pl.pallas_call(
        kern,
        out_shape=out_shape,
        grid=(m // tm, nk),
        in_specs=[pl.BlockSpec((tm, tk), lambda i, k: (i, k)),
                  pl.BlockSpec((tk, n), lambda i, k: (k, 0)),
                  pl.BlockSpec((tm, n), lambda i, k: (i, 0), pipeline_mode=pl.Buffered(1)),
                  pl.BlockSpec((1, n), lambda i, k: (0, 0)),
                  pl.BlockSpec((1, n), lambda i, k: (0, 0))],
        out_specs=out_specs,
        compiler_params=_params("parallel", "arbitrary"),
        name="matmul_resid_norm",
    )(a, w, x, g_post.reshape(1, n), gn.reshape(1, n))
    return (outs[0], outs[1]) if with_next else (outs[0], None)


def _ab_block(go, gi, val, ga, gg, ca_halo, cb_halo, aw_ref, bw_ref, bias, lng, lnb,
              exta_ref, extb_ref, z_ref):
    r, c = go.shape
    ca = gi * val
    exta_ref[0:HALO_A, :] = ca_halo
    exta_ref[HALO_A:HALO_A + r, :] = ca
    conv_a = aw_ref[CONV_A - 1:CONV_A, :] * ca
    for j in range(CONV_A - 1):
        off = HALO_A - (CONV_A - 1) + j
        conv_a = conv_a + aw_ref[j:j + 1, :] * exta_ref[off:off + r, :]
    y_a = go * conv_a
    cb = ga * jax.nn.sigmoid(gg)
    extb_ref[0:HALO_B, :] = cb_halo
    extb_ref[HALO_B:HALO_B + r, :] = cb
    rc = min(r, 32)
    cc = min(c, 512)
    base = HALO_B - (CONV_B - 1)
    for r0 in range(0, r, rc):
        for c0 in range(0, c, cc):
            acc = jnp.broadcast_to(bias[:, c0:c0 + cc], (rc, cc))
            for j in range(CONV_B):
                acc = acc + bw_ref[j:j + 1, c0:c0 + cc] * extb_ref[base + j + r0:base + j + r0 + rc, c0:c0 + cc]
            z_ref[r0:r0 + rc, c0:c0 + cc] = acc
    y_b = _silu(_layer_norm(z_ref[0:r, :], lng, lnb))
    return y_a, y_b, ca, cb


def _ab_prompt_kernel(go_ref, gi_ref, val_ref, ga_ref, gg_ref, gih_ref, valh_ref, gah_ref, ggh_ref,
                      aw_ref, bw_ref, bias_ref, lng_ref, lnb_ref,
                      y_ref, ca_tail_ref, cb_tail_ref, exta_ref, extb_ref, z_ref, *, n_i):
    i = pl.program_id(1)
    r, c = go_ref.shape
    first = i == 0
    ca_halo = jnp.where(first, 0.0, gih_ref[...] * valh_ref[...])
    cb_halo = jnp.where(first, 0.0, gah_ref[...] * jax.nn.sigmoid(ggh_ref[...]))
    y_a, y_b, ca, cb = _ab_block(go_ref[...], gi_ref[...], val_ref[...], ga_ref[...], gg_ref[...],
                                 ca_halo, cb_halo, aw_ref, bw_ref, bias_ref[...], lng_ref[...], lnb_ref[...],
                                 exta_ref, extb_ref, z_ref)
    y_ref[:, 0:c] = y_a.astype(BF16)
    y_ref[:, c:2 * c] = y_b.astype(BF16)

    @pl.when(i == n_i - 1)
    def _():
        ca_tail_ref[0] = exta_ref[r:r + HALO_A, :]
        cb_tail_ref[0] = extb_ref[r:r + HALO_B, :]


def mixer_ab_prompt(secs, n_b, t, a_w, b_w, b_bias, ln_g, ln_b):
    go, gi, val, ga, gg = secs
    c = go.shape[1]
    tr = _tile(t, 256, HALO_B)
    n_i = t // tr
    cur = pl.BlockSpec((tr, c), lambda b, i: (b * n_i + i, 0))

    def halo(rows):
        per = tr // rows
        return pl.BlockSpec((rows, c), lambda b, i: (jnp.maximum((b * n_i + i) * per - 1, 0), 0))

    def vec(rows):
        return pl.BlockSpec((rows, c), lambda b, i: (0, 0))

    kern = functools.partial(_ab_prompt_kernel, n_i=n_i)
    return pl.pallas_call(
        kern,
        out_shape=[jax.ShapeDtypeStruct((n_b * t, 2 * c), BF16),
                   jax.ShapeDtypeStruct((n_b, HALO_A, c), F32),
                   jax.ShapeDtypeStruct((n_b, HALO_B, c), F32)],
        grid=(n_b, n_i),
        in_specs=[cur, cur, cur, cur, cur, halo(HALO_A), halo(HALO_A), halo(HALO_B), halo(HALO_B),
                  vec(CONV_A), vec(CONV_B), vec(1), vec(1), vec(1)],
        out_specs=[pl.BlockSpec((tr, 2 * c), lambda b, i: (b * n_i + i, 0)),
                   pl.BlockSpec((1, HALO_A, c), lambda b, i: (b, 0, 0)),
                   pl.BlockSpec((1, HALO_B, c), lambda b, i: (b, 0, 0))],
        scratch_shapes=[pltpu.VMEM((HALO_A + tr, c), F32), pltpu.VMEM((HALO_B + tr, c), F32),
                        pltpu.VMEM((tr, c), F32)],
        compiler_params=_params("parallel", "arbitrary"),
        name="mixer_ab_prompt",
    )(go, gi, val, ga, gg, gi, val, ga, gg, a_w, b_w, b_bias.reshape(1, c), ln_g.reshape(1, c), ln_b.reshape(1, c))


def _ab_sample_kernel(go_ref, gi_ref, val_ref, ga_ref, gg_ref, ha_ref, hb_ref,
                      aw_ref, bw_ref, bias_ref, lng_ref, lnb_ref,
                      y_ref, ca_tail_ref, cb_tail_ref, exta_ref, extb_ref, z_ref, *, s):
    nb = ha_ref.shape[0]
    c = go_ref.shape[1]
    def one_stream(q, carry):
        rows = pl.ds(pl.multiple_of(q * s, s), s)
        y_a, y_b, ca, cb = _ab_block(go_ref[rows, :], gi_ref[rows, :], val_ref[rows, :], ga_ref[rows, :],
                                     gg_ref[rows, :], ha_ref[q], hb_ref[q], aw_ref, bw_ref, bias_ref[...],
                                     lng_ref[...], lnb_ref[...], exta_ref, extb_ref, z_ref)
        y_ref[rows, 0:c] = y_a.astype(BF16)
        y_ref[rows, c:2 * c] = y_b.astype(BF16)
        ca_tail_ref[q] = exta_ref[s:s + HALO_A, :]
        cb_tail_ref[q] = extb_ref[s:s + HALO_B, :]
        return carry

    lax.fori_loop(0, nb, one_stream, 0)


def mixer_ab_sample(secs, row0, n_b, s, hist_a, hist_b, a_w, b_w, b_bias, ln_g, ln_b):
    go, gi, val, ga, gg = secs
    c = go.shape[1]
    assert s >= HALO_B
    nb = _tile(n_b, 8, 1)
    assert row0 % (nb * s) == 0
    blk0 = row0 // (nb * s)
    ha = jnp.pad(hist_a, ((0, 0), (HALO_A - (CONV_A - 1), 0), (0, 0)))
    hb = jnp.pad(hist_b, ((0, 0), (HALO_B - (CONV_B - 1), 0), (0, 0)))
    cur = pl.BlockSpec((nb * s, c), lambda i: (blk0 + i, 0))

    def vec(rows):
        return pl.BlockSpec((rows, c), lambda i: (0, 0))

    kern = functools.partial(_ab_sample_kernel, s=s)
    return pl.pallas_call(
        kern,
        out_shape=[jax.ShapeDtypeStruct((n_b * s, 2 * c), BF16),
                   jax.ShapeDtypeStruct((n_b, HALO_A, c), F32),
                   jax.ShapeDtypeStruct((n_b, HALO_B, c), F32)],
        grid=(n_b // nb,),
        in_specs=[cur, cur, cur, cur, cur,
                  pl.BlockSpec((nb, HALO_A, c), lambda i: (i, 0, 0)),
                  pl.BlockSpec((nb, HALO_B, c), lambda i: (i, 0, 0)),
                  vec(CONV_A), vec(CONV_B), vec(1), vec(1), vec(1)],
        out_specs=[pl.BlockSpec((nb * s, 2 * c), lambda i: (i, 0)),
                   pl.BlockSpec((nb, HALO_A, c), lambda i: (i, 0, 0)),
                   pl.BlockSpec((nb, HALO_B, c), lambda i: (i, 0, 0))],
        scratch_shapes=[pltpu.VMEM((HALO_A + s, c), F32), pltpu.VMEM((HALO_B + s, c), F32),
                        pltpu.VMEM((s, c), F32)],
        compiler_params=_params("parallel"),
        name="mixer_ab_sample",
    )(go, gi, val, ga, gg, ha, hb, a_w, b_w, b_bias.reshape(1, c), ln_g.reshape(1, c), ln_b.reshape(1, c))


def _rope_tables(pos):
    half = HEAD_DIM // 2
    inv_freq = ROPE_THETA ** (-jnp.arange(half, dtype=F32) / half)
    ang = pos.astype(F32)[:, None] * inv_freq[None, :]
    cos = jnp.cos(ang)
    sin = jnp.sin(ang)
    return jnp.concatenate([cos, cos], axis=-1), jnp.concatenate([-sin, sin], axis=-1)


def _rope(x, cos2, sin2):
    return x * cos2 + pltpu.roll(x, HEAD_DIM // 2, 1) * sin2


def _sink_softmax_pv(s, sink, vv):
    m = jnp.maximum(jnp.max(s, axis=-1, keepdims=True), sink)
    e = jnp.exp(s - m)
    p = e / (jnp.sum(e, axis=-1, keepdims=True) + jnp.exp(sink - m))
    return jnp.dot(p.astype(BF16), vv, preferred_element_type=F32)


def _qk(q, kk):
    return lax.dot_general(q, kk, (((1,), (1,)), ((), ())), preferred_element_type=F32) * (HEAD_DIM ** -0.5)


def _attn_prompt_kernel(sink_ref, q_ref, k_ref, v_ref, kh_ref, vh_ref, cos_ref, sin_ref, cosh_ref, sinh_ref,
                        att_ref, krot_ref):
    i = pl.program_id(1)
    tq = q_ref.shape[0]
    cos, sin = cos_ref[...], sin_ref[...]
    cosh, sinh = cosh_ref[...], sinh_ref[...]
    n_keys = WINDOW + tq
    q_chunk = lax.broadcasted_iota(jnp.int32, (tq, n_keys), 0) // CHUNK
    k_col = lax.broadcasted_iota(jnp.int32, (tq, n_keys), 1)
    k_chunk = k_col // CHUNK
    n_back = WINDOW // CHUNK
    mask = (k_chunk >= q_chunk) & (k_chunk <= q_chunk + n_back)
    mask = mask & (k_col >= jnp.where(i > 0, 0, WINDOW))
    for h in range(N_KV_HEADS):
        hs = slice(h * HEAD_DIM, (h + 1) * HEAD_DIM)
        k_cur = _rope(k_ref[:, hs], cos, sin)
        krot_ref[:, hs] = k_cur
        kk = jnp.concatenate([_rope(kh_ref[:, hs], cosh, sinh), k_cur], axis=0).astype(BF16)
        vv = jnp.concatenate([vh_ref[:, hs], v_ref[:, hs]], axis=0).astype(BF16)
        for g in range(GQA_GROUP):
            n = h * GQA_GROUP + g
            ns = slice(n * HEAD_DIM, (n + 1) * HEAD_DIM)
            qn = _rope(q_ref[:, ns], cos, sin).astype(BF16)
            s = jnp.where(mask, _qk(qn, kk), -jnp.inf)
            att_ref[:, ns] = _sink_softmax_pv(s, sink_ref[n], vv).astype(BF16)


def attn_prompt(q, k, v, sink, n_b, t):
    qw, kw = q.shape[1], k.shape[1]
    tq = _tile(t, 256, WINDOW)
    n_i = t // tq
    per = tq // WINDOW
    cos2, sin2 = _rope_tables(jnp.arange(t, dtype=jnp.int32))
    cur = lambda w: pl.BlockSpec((tq, w), lambda b, i: (b * n_i + i, 0))
    halo = pl.BlockSpec((WINDOW, kw), lambda b, i: (jnp.maximum((b * n_i + i) * per - 1, 0), 0))
    tab = pl.BlockSpec((tq, HEAD_DIM), lambda b, i: (i, 0))
    tab_h = pl.BlockSpec((WINDOW, HEAD_DIM), lambda b, i: (jnp.maximum(i * per - 1, 0), 0))
    return pl.pallas_call(
        _attn_prompt_kernel,
        out_shape=[jax.ShapeDtypeStruct((n_b * t, qw), BF16), jax.ShapeDtypeStruct((n_b * t, kw), F32)],
        grid=(n_b, n_i),
        in_specs=[pl.BlockSpec(memory_space=pltpu.SMEM), cur(qw), cur(kw), cur(kw), halo, halo, tab, tab, tab_h, tab_h],
        out_specs=[cur(qw), cur(kw)],
        compiler_params=_params("parallel", "parallel"),
        name="attn_prompt",
    )(sink.reshape(-1), q, k, v, k, v, cos2, sin2, cos2, sin2)


def _attn_sample_kernel(sink_ref, q_ref, k_ref, v_ref, wk_ref, wv_ref, cos_ref, sin_ref, att_ref, krot_ref, *, s):
    nb = wk_ref.shape[0]
    cos, sin = cos_ref[...], sin_ref[...]
    def one_stream(b, carry):
        rows = pl.ds(pl.multiple_of(b * s, s), s)
        for h in range(N_KV_HEADS):
            hs = slice(h * HEAD_DIM, (h + 1) * HEAD_DIM)
            k_cur = _rope(k_ref[rows, hs], cos, sin)
            krot_ref[rows, hs] = k_cur
            kk = jnp.concatenate([wk_ref[b, :, hs], k_cur], axis=0).astype(BF16)
            vv = jnp.concatenate([wv_ref[b, :, hs], v_ref[rows, hs]], axis=0).astype(BF16)
            for g in range(GQA_GROUP):
                n = h * GQA_GROUP + g
                ns = slice(n * HEAD_DIM, (n + 1) * HEAD_DIM)
                qn = _rope(q_ref[rows, ns], cos, sin).astype(BF16)
                att_ref[rows, ns] = _sink_softmax_pv(_qk(qn, kk), sink_ref[n], vv).astype(BF16)
        return carry

    lax.fori_loop(0, nb, one_stream, 0)


def attn_sample(q, k, v, row0, n_b, s, win_k, win_v, sink):
    qw, kw = q.shape[1], k.shape[1]
    nb = _tile(n_b, 8, 1)
    assert row0 % (nb * s) == 0
    blk0 = row0 // (nb * s)
    cos2, sin2 = _rope_tables(PAST_LEN + jnp.arange(s, dtype=jnp.int32))
    cur = lambda w: pl.BlockSpec((nb * s, w), lambda i: (blk0 + i, 0))
    out = lambda w: pl.BlockSpec((nb * s, w), lambda i: (i, 0))
    win = pl.BlockSpec((nb, WINDOW, kw), lambda i: (i, 0, 0))
    tab = pl.BlockSpec((s, HEAD_DIM), lambda i: (0, 0))
    kern = functools.partial(_attn_sample_kernel, s=s)
    return pl.pallas_call(
        kern,
        out_shape=[jax.ShapeDtypeStruct((n_b * s, qw), BF16), jax.ShapeDtypeStruct((n_b * s, kw), F32)],
        grid=(n_b // nb,),
        in_specs=[pl.BlockSpec(memory_space=pltpu.SMEM), cur(qw), cur(kw), cur(kw), win, win, tab, tab],
        out_specs=[out(qw), out(kw)],
        compiler_params=_params("parallel"),
        name="attn_sample",
    )(sink.reshape(-1), q, k, v, win_k.reshape(n_b, WINDOW, kw), win_v.reshape(n_b, WINDOW, kw), cos2, sin2)


def _gmlp_kernel(u_ref, vg_ref, lng_ref, lnb_ref, ws_ref, bst_ref, y_ref, *rest, with_vn):
    tr, c = u_ref.shape
    r = ws_ref.shape[1]
    gd = c // GMLP_GROUPS
    vn = _layer_norm(vg_ref[...], lng_ref[...], lnb_ref[...])
    if with_vn:
        rest[0][...] = vn
    vnb = vn.astype(BF16)
    row_chunk = lax.broadcasted_iota(jnp.int32, (r, r), 0) // CHUNK
    col_chunk = lax.broadcasted_iota(jnp.int32, (r, r), 1) // CHUNK
    allowed = col_chunk <= row_chunk
    for g in range(GMLP_GROUPS):
        wg = jnp.where(allowed, ws_ref[g], 0.0).astype(BF16)
        bias = bst_ref[:, g:g + 1]
        cs = slice(g * gd, (g + 1) * gd)
        for r0 in range(0, tr, r):
            sg = jnp.dot(wg, vnb[r0:r0 + r, cs], preferred_element_type=F32) + bias
            y_ref[r0:r0 + r, cs] = (u_ref[r0:r0 + r, cs] * sg).astype(BF16)


def gmlp(u, vg, row0, n_rows, r, ln_g, ln_b, w_s, b_s, with_vn):
    c = u.shape[1]
    tr = _tile(n_rows, 256, r)
    assert row0 % tr == 0
    blk0 = row0 // tr
    ws = w_s[:, :r, :r]
    bst = b_s[:, :r].T
    cur = pl.BlockSpec((tr, c), lambda i: (blk0 + i, 0))
    out = pl.BlockSpec((tr, c), lambda i: (i, 0))
    vec = pl.BlockSpec((1, c), lambda i: (0, 0))
    out_shape = [jax.ShapeDtypeStruct((n_rows, c), BF16)]
    out_specs = [out]
    if with_vn:
        out_shape.append(jax.ShapeDtypeStruct((n_rows, c), F32))
        out_specs.append(out)
    kern = functools.partial(_gmlp_kernel, with_vn=with_vn)
    outs = pl.pallas_call(
        kern,
        out_shape=out_shape,
        grid=(n_rows // tr,),
        in_specs=[cur, cur, vec, vec,
                  pl.BlockSpec((GMLP_GROUPS, r, r), lambda i: (0, 0, 0)),
                  pl.BlockSpec((r, GMLP_GROUPS), lambda i: (0, 0))],
        out_specs=out_specs,
        compiler_params=_params("parallel"),
        name="gmlp",
    )(u, vg, ln_g.reshape(1, c), ln_b.reshape(1, c), ws, bst)
    return (outs[0], outs[1]) if with_vn else (outs[0], None)


def _xattn_kernel(q_ref, mk_ref, mv_ref, o_ref, *, tq):
    nb = mk_ref.shape[0]
    for b in range(nb):
        rows = slice(b * tq, (b + 1) * tq)
        for h in range(N_X_HEADS):
            hs = slice(h * HEAD_DIM, (h + 1) * HEAD_DIM)
            s = _qk(q_ref[rows, hs].astype(BF16), mk_ref[b, :, hs].astype(BF16))
            m = jnp.max(s, axis=-1, keepdims=True)
            e = jnp.exp(s - m)
            p = e / jnp.sum(e, axis=-1, keepdims=True)
            o = jnp.dot(p.astype(BF16), mv_ref[b, :, hs].astype(BF16), preferred_element_type=F32)
            o_ref[rows, hs] = o.astype(BF16)


def cross_attn(q, row0, n_b, t, mk, mv):
    w = q.shape[1]
    n_mem = mk.shape[1]
    if t >= 512:
        nb, tq = 1, _tile(t, 512, SUBLANE)
    else:
        nb, tq = _tile(n_b, 8, 1), t
    n_i = t // tq
    assert row0 % (nb * tq) == 0
    blk0 = row0 // (nb * tq)
    kern = functools.partial(_xattn_kernel, tq=tq)
    mem = pl.BlockSpec((nb, n_mem, w), lambda b, i: (b, 0, 0))
    return pl.pallas_call(
        kern,
        out_shape=jax.ShapeDtypeStruct((n_b * t, w), BF16),
        grid=(n_b // nb, n_i),
        in_specs=[pl.BlockSpec((nb * tq, w), lambda b, i: (blk0 + b * n_i + i, 0)), mem, mem],
        out_specs=pl.BlockSpec((nb * tq, w), lambda b, i: (b * n_i + i, 0)),
        compiler_params=_params("parallel", "parallel"),
        name="cross_attn",
    )(q, mk, mv)


def _bf16_weight(w, pad_rows=0, pad_cols=0):
    w = w.astype(BF16)
    if pad_rows or pad_cols:
        w = jnp.pad(w, ((0, pad_rows), (0, pad_cols)))
    return w


def _half_ffn(h, x, w_gate, w_up, w_down, g_post, g_next):
    f = w_gate.shape[1]
    pad = -f % FFN_PAD
    a = swiglu_up(h, _bf16_weight(w_gate, 0, pad), _bf16_weight(w_up, 0, pad))
    return matmul_resid_norm(a, _bf16_weight(w_down, pad, 0), x, g_post, g_next, 0.5)


def kernel(x_prompt, x_sample, mem_prompt, state_conv_a, state_conv_b, cache_win_k, cache_win_v, cache_mem_k, cache_mem_v, norms, ffn_w_gate, ffn_w_up, ffn_w_down, xattn_wq, xattn_wk, xattn_wv, xattn_wo, ab_w_in, a_conv_w, b_conv_w, b_conv_bias, b_ln_g, b_ln_b, ab_w_out, cd_w_in, c_sink, d_ln_g, d_ln_b, d_w_s, d_b_s, cd_w_out):
    n_bp, t, d = x_prompt.shape
    n_bs, s, _ = x_sample.shape
    depth = norms.shape[0]
    mp, ms = n_bp * t, n_bs * s
    n_mem = mem_prompt.shape[1]
    wx = xattn_wq.shape[2]

    x = jnp.concatenate([x_prompt.reshape(mp, d), x_sample.reshape(ms, d)], axis=0)
    mem = mem_prompt.reshape(n_bp * n_mem, d)
    h = rms_cast(x, norms[0, 0])

    conv_a_p, conv_a_s, conv_b_p, conv_b_s = [], [], [], []
    win_k_p, win_v_p, win_k_s, win_v_s, chunk_v_s = [], [], [], [], []
    mem_k_p, mem_v_p = [], []
    for layer in range(depth):
        g = norms[layer]
        x, h = _half_ffn(h, x, ffn_w_gate[layer, 0], ffn_w_up[layer, 0], ffn_w_down[layer, 0], g[1], g[2])

        if layer % 2 == 0:
            e = layer // 2
            c = a_conv_w.shape[2]
            w_in = _bf16_weight(ab_w_in[e])
            secs = [matmul_cols(h, w_in, j * c, c) for j in range(5)]
            conv = (a_conv_w[e], b_conv_w[e], b_conv_bias[e], b_ln_g[e], b_ln_b[e])
            y_p, ca_p, cb_p = mixer_ab_prompt(secs, n_bp, t, *conv)
            y_s, ca_s, cb_s = mixer_ab_sample(secs, mp, n_bs, s, state_conv_a[e], state_conv_b[e], *conv)
            conv_a_p.append(ca_p[:, HALO_A - (CONV_A - 1):])
            conv_a_s.append(ca_s[:, HALO_A - (CONV_A - 1):])
            conv_b_p.append(cb_p[:, HALO_B - (CONV_B - 1):])
            conv_b_s.append(cb_s[:, HALO_B - (CONV_B - 1):])
            y = jnp.concatenate([y_p, y_s], axis=0)
            w_out = _bf16_weight(ab_w_out[e])
        else:
            o = layer // 2
            w_in = _bf16_weight(cd_w_in[o])
            wd_ = d_ln_g.shape[1]
            kvw = N_KV_HEADS * HEAD_DIM
            qw = w_in.shape[1] - 2 * kvw - 2 * wd_
            q = matmul_cols(h, w_in, 0, qw)
            k = matmul_cols(h, w_in, qw, kvw)
            v = matmul_cols(h, w_in, qw + kvw, kvw)
            u = matmul_cols(h, w_in, qw + 2 * kvw, wd_)
            vg = matmul_cols(h, w_in, qw + 2 * kvw + wd_, wd_)
            att_p, krot_p = attn_prompt(q, k, v, c_sink[o], n_bp, t)
            att_s, krot_s = attn_sample(q, k, v, mp, n_bs, s, cache_win_k[o], cache_win_v[o], c_sink[o])
            yd_p, _ = gmlp(u, vg, 0, mp, GMLP_CHUNK, d_ln_g[o], d_ln_b[o], d_w_s[o], d_b_s[o], False)
            yd_s, vn_s = gmlp(u, vg, mp, ms, s, d_ln_g[o], d_ln_b[o], d_w_s[o], d_b_s[o], True)
            kv_shape = (N_KV_HEADS, HEAD_DIM)
            win_k_p.append(krot_p.reshape(n_bp, t, *kv_shape)[:, -WINDOW:])
            win_v_p.append(v[:mp].reshape(n_bp, t, *kv_shape)[:, -WINDOW:])
            k_all = jnp.concatenate([cache_win_k[o], krot_s.reshape(n_bs, s, *kv_shape)], axis=1)
            v_all = jnp.concatenate([cache_win_v[o], v[mp:].reshape(n_bs, s, *kv_shape)], axis=1)
            win_k_s.append(k_all[:, -WINDOW:])
            win_v_s.append(v_all[:, -WINDOW:])
            chunk_v_s.append(vn_s.reshape(n_bs, s, wd_))
            y = jnp.concatenate([jnp.concatenate([att_p, yd_p], axis=1),
                                 jnp.concatenate([att_s, yd_s], axis=1)], axis=0)
            w_out = _bf16_weight(cd_w_out[o])
        x, h = matmul_resid_norm(y, w_out, x, g[3], g[4], 1.0)

        m_n = rms_cast(mem, g[8])
        mk = matmul_cols(m_n, _bf16_weight(xattn_wk[layer]), 0, wx)
        mv = matmul_cols(m_n, _bf16_weight(xattn_wv[layer]), 0, wx)
        mem_k_p.append(mk.reshape(n_bp, n_mem, N_X_HEADS, HEAD_DIM))
        mem_v_p.append(mv.reshape(n_bp, n_mem, N_X_HEADS, HEAD_DIM))
        qx = matmul_cols(h, _bf16_weight(xattn_wq[layer]), 0, wx)
        o_p = cross_attn(qx, 0, n_bp, t, mk.reshape(n_bp, n_mem, wx), mv.reshape(n_bp, n_mem, wx))
        o_s = cross_attn(qx, mp, n_bs, s, cache_mem_k[layer].reshape(n_bs, n_mem, wx),
                         cache_mem_v[layer].reshape(n_bs, n_mem, wx))
        x, h = matmul_resid_norm(jnp.concatenate([o_p, o_s], axis=0), _bf16_weight(xattn_wo[layer]),
                                 x, g[5], g[6], 1.0)

        g_next = norms[layer + 1, 0] if layer + 1 < depth else None
        x, h = _half_ffn(h, x, ffn_w_gate[layer, 1], ffn_w_up[layer, 1], ffn_w_down[layer, 1], g[7], g_next)

    return (x[:mp].reshape(n_bp, t, d), x[mp:].reshape(n_bs, s, d),
            jnp.stack(conv_a_p), jnp.stack(conv_a_s), jnp.stack(conv_b_p), jnp.stack(conv_b_s),
            jnp.stack(win_k_p), jnp.stack(win_v_p), jnp.stack(win_k_s), jnp.stack(win_v_s),
            jnp.stack(chunk_v_s), jnp.stack(mem_k_p), jnp.stack(mem_v_p))
```

```python
import functools

import jax
import jax.numpy as jnp
from jax import lax
from jax.experimental import pallas as pl
from jax.experimental.pallas import tpu as pltpu

EPS = 1e-6
CHUNK = 64
HEAD_DIM = 128
WINDOW = 128
ROPE_THETA = 10000.0
GMLP_CHUNK = 128
GMLP_GROUPS = 8
N_KV_HEADS = 4
GQA_GROUP = 4
N_X_HEADS = 4
CONV_A = 3
CONV_B = 31
PAST_LEN = 2048
N_NORMS = 9

V7X_VMEM_BYTES = 64 * 1024 * 1024
VMEM_LIMIT = V7X_VMEM_BYTES - 8 * 1024 * 1024
SUBLANE = 8
LANE = 128
HALO_A = SUBLANE
HALO_B = 4 * SUBLANE

BF16 = jnp.bfloat16
F32 = jnp.float32


def _tile(n, target, mult):
    t = min(n, target)
    t -= t % mult
    while t >= mult:
        if n % t == 0:
            return t
        t -= mult
    return n


def _params(*sem):
    return pltpu.CompilerParams(dimension_semantics=sem, vmem_limit_bytes=VMEM_LIMIT)


def _rms(x, g):
    return x * lax.rsqrt(jnp.mean(x * x, axis=-1, keepdims=True) + EPS) * g


def _layer_norm(x, g, b):
    mu = jnp.mean(x, axis=-1, keepdims=True)
    xc = x - mu
    var = jnp.mean(xc * xc, axis=-1, keepdims=True)
    return xc * lax.rsqrt(var + EPS) * g + b


def _silu(x):
    return x * jax.nn.sigmoid(x)


def _rms_cast2_kernel(xp_ref, xs_ref, g_ref, x_ref, h_ref, *, n_p):
    i = pl.program_id(0)

    @pl.when(i < n_p)
    def _():
        x = xp_ref[...]
        x_ref[...] = x
        h_ref[...] = _rms(x, g_ref[...]).astype(BF16)

    @pl.when(i >= n_p)
    def _():
        x = xs_ref[...]
        x_ref[...] = x
        h_ref[...] = _rms(x, g_ref[...]).astype(BF16)


def rms_cast_join(xp, xs, g):
    (mp, d), ms = xp.shape, xs.shape[0]
    tm = _tile(ms, 256, SUBLANE)
    assert mp % tm == 0
    n_p, n_s = mp // tm, ms // tm
    kern = functools.partial(_rms_cast2_kernel, n_p=n_p)
    row = lambda i: (i, 0)
    return pl.pallas_call(
        kern,
        out_shape=[jax.ShapeDtypeStruct((mp + ms, d), F32), jax.ShapeDtypeStruct((mp + ms, d), BF16)],
        grid=(n_p + n_s,),
        in_specs=[pl.BlockSpec((tm, d), lambda i: (jnp.minimum(i, n_p - 1), 0)),
                  pl.BlockSpec((tm, d), lambda i: (jnp.maximum(i - n_p, 0), 0)),
                  pl.BlockSpec((1, d), lambda i: (0, 0))],
        out_specs=[pl.BlockSpec((tm, d), row), pl.BlockSpec((tm, d), row)],
        compiler_params=_params("parallel"),
        name="rms_cast_join",
    )(xp, xs, g.reshape(1, d))


def _rms_cast_kernel(x_ref, g_ref, o_ref):
    o_ref[...] = _rms(x_ref[...], g_ref[...]).astype(o_ref.dtype)


def rms_cast(x, g):
    m, d = x.shape
    tm = _tile(m, 256, SUBLANE)
    return pl.pallas_call(
        _rms_cast_kernel,
        out_shape=jax.ShapeDtypeStruct((m, d), BF16),
        grid=(m // tm,),
        in_specs=[pl.BlockSpec((tm, d), lambda i: (i, 0)),
                  pl.BlockSpec((1, d), lambda i: (0, 0))],
        out_specs=pl.BlockSpec((tm, d), lambda i: (i, 0)),
        compiler_params=_params("parallel"),
        name="rms_cast",
    )(x, g.reshape(1, d))


def _mm_kernel(a_ref, w_ref, o_ref):
    w = w_ref[...].astype(BF16)
    o_ref[...] = jnp.dot(a_ref[...], w, preferred_element_type=F32).astype(o_ref.dtype)


def matmul_cols(a, w, e, col0, ncols, out_dtype=F32):
    m, k = a.shape
    tm = _tile(m, 1024, SUBLANE)
    tn = _tile(ncols, 512, LANE)
    assert col0 % tn == 0
    cb = col0 // tn
    return pl.pallas_call(
        _mm_kernel,
        out_shape=jax.ShapeDtypeStruct((m, ncols), out_dtype),
        grid=(m // tm, ncols // tn),
        in_specs=[pl.BlockSpec((tm, k), lambda i, j: (i, 0)),
                  pl.BlockSpec((None, k, tn), lambda i, j: (e, 0, cb + j))],
        out_specs=pl.BlockSpec((tm, tn), lambda i, j: (i, j)),
        compiler_params=_params("parallel", "parallel"),
        name="matmul_cols",
    )(a, w)


def _swiglu_kernel(h_ref, wg_ref, wu_ref, o_ref):
    h = h_ref[...]
    g = jnp.dot(h, wg_ref[...].astype(BF16), preferred_element_type=F32)
    u = jnp.dot(h, wu_ref[...].astype(BF16), preferred_element_type=F32)
    o_ref[...] = (_silu(g) * u).astype(o_ref.dtype)


def swiglu_up(h, wg, wu, e):
    m, k = h.shape
    f = wg.shape[2]
    tm = _tile(m, 1024, SUBLANE)
    tn = _tile(f, 256, LANE)
    wspec = pl.BlockSpec((None, k, tn), lambda i, j: (e, 0, j))
    return pl.pallas_call(
        _swiglu_kernel,
        out_shape=jax.ShapeDtypeStruct((m, f), BF16),
        grid=(m // tm, f // tn),
        in_specs=[pl.BlockSpec((tm, k), lambda i, j: (i, 0)), wspec, wspec],
        out_specs=pl.BlockSpec((tm, tn), lambda i, j: (i, j)),
        compiler_params=_params("parallel", "parallel"),
        name="swiglu_up",
    )(h, wg, wu)


EPILOGUE_ROWS = 2 * SUBLANE
EPILOGUE_UNROLL = 4


def _mm_resid_kernel(*refs, n_main, has_tail, n_chunk, with_next):
    a_ref, w_ref = refs[:2]
    refs = refs[2:]
    if has_tail:
        at_ref, wt_ref = refs[:2]
        refs = refs[2:]
    x_ref, gpost_ref, gnext_ref, xo_ref = refs[:4]
    rest = refs[4:]
    k = pl.program_id(1)
    tm, n = xo_ref.shape
    nk = n_main + int(has_tail)

    def accumulate(a_blk, w_blk, first):
        a = a_blk[...]
        for c in range(0, n, n_chunk):
            part = jnp.dot(a, w_blk[:, c:c + n_chunk], preferred_element_type=F32)
            if first:
                xo_ref[:, c:c + n_chunk] = part
            else:
                xo_ref[:, c:c + n_chunk] += part

    @pl.when(k == 0)
    def _():
        accumulate(a_ref, w_ref, True)

    if n_main > 1:
        @pl.when((k > 0) & (k < n_main))
        def _():
            accumulate(a_ref, w_ref, False)

    if has_tail:
        @pl.when(k == n_main)
        def _():
            accumulate(at_ref, wt_ref, False)

    @pl.when(k == nk - 1)
    def _():
        inv_ref = rest[-1]
        lanes = [slice(j, j + LANE) for j in range(0, n, LANE)]

        def rows_of(c):
            if isinstance(c, int):
                return pl.ds(c * EPILOGUE_ROWS, EPILOGUE_ROWS)
            return pl.ds(pl.multiple_of(c * EPILOGUE_ROWS, EPILOGUE_ROWS), EPILOGUE_ROWS)

        def inv_rms(sq_lanes):
            ms = jnp.sum(sq_lanes, axis=-1, keepdims=True) * (1.0 / n)
            return jnp.broadcast_to(lax.rsqrt(ms + EPS), (EPILOGUE_ROWS, LANE))

        def stats_rows(c):
            rows = rows_of(c)
            sq = jnp.zeros((EPILOGUE_ROWS, LANE), F32)
            for ls in lanes:
                f = xo_ref[rows, ls]
                sq = sq + f * f
            return sq

        def resid_rows(c):
            rows = rows_of(c)
            r = inv_ref[rows, :]
            sq = jnp.zeros((EPILOGUE_ROWS, LANE), F32)
            for ls in lanes:
                xn = x_ref[rows, ls] + xo_ref[rows, ls] * r * gpost_ref[:, ls]
                xo_ref[rows, ls] = xn
                sq = sq + xn * xn
            return sq

        def next_step(c, carry):
            rows = rows_of(c)
            r = inv_ref[rows, :]
            for ls in lanes:
                rest[0][rows, ls] = (xo_ref[rows, ls] * r * gnext_ref[:, ls]).astype(BF16)
            return carry

        steps = tm // EPILOGUE_ROWS
        u = EPILOGUE_UNROLL
        assert steps % u == 0

        def sweep(rows_fn, keep_stats):
            def batch(b):
                return tuple(rows_fn(b * u + j) for j in range(u))

            def finish(b, sqs):
                if keep_stats:
                    for j in range(u):
                        inv_ref[rows_of(b * u + j), :] = inv_rms(sqs[j])

            def step(b, sqs_prev):
                sqs = batch(b)
                finish(b - 1, sqs_prev)
                return sqs

            n_batches = steps // u
            finish(n_batches - 1, lax.fori_loop(1, n_batches, step, batch(0)))

        sweep(stats_rows, True)
        sweep(resid_rows, with_next)
        if with_next:
            lax.fori_loop(0, steps, next_step, 0, unroll=u)


def matmul_resid_norm(a, w, e, x, g_post, g_next, scale):
    m, kdim = a.shape
    n = w.shape[2]
    tm = _tile(m, 512, EPILOGUE_ROWS)
    tk = min(512, kdim)
    n_main, tail = kdim // tk, kdim % tk
    has_tail = tail > 0
    nk = n_main + int(has_tail)
    with_next = g_next is not None
    gn = g_next if with_next else g_post
    gp = g_post if scale == 1.0 else g_post * scale
    row = lambda i, k: (i, 0)
    fixed = lambda i, k: (0, 0)
    out_shape = [jax.ShapeDtypeStruct((m, n), F32)]
    out_specs = [pl.BlockSpec((tm, n), row)]
    if with_next:
        out_shape.append(jax.ShapeDtypeStruct((m, n), BF16))
        out_specs.append(pl.BlockSpec((tm, n), row))
    kern = functools.partial(_mm_resid_kernel, n_main=n_main, has_tail=has_tail,
                             n_chunk=_tile(n, 1024, LANE), with_next=with_next)
    in_specs = [pl.BlockSpec((tm, tk), lambda i, k: (i, jnp.minimum(k, n_main - 1))),
                pl.BlockSpec((None, tk, n), lambda i, k: (e, jnp.minimum(k, n_main - 1), 0))]
    operands = [a, w]
    if has_tail:
        assert tail % LANE == 0 and (n_main * tk) % tail == 0
        tail_blk = (n_main * tk) // tail
        in_specs += [pl.BlockSpec((tm, tail), lambda i, k: (i, tail_blk)),
                     pl.BlockSpec((None, tail, n), lambda i, k: (e, tail_blk, 0))]
        operands += [a, w]
    in_specs += [pl.BlockSpec((tm, n), row), pl.BlockSpec((1, n), fixed), pl.BlockSpec((1, n), fixed)]
    operands += [x, gp.reshape(1, n), gn.reshape(1, n)]
    outs = pl.pallas_call(
        kern,
        out_shape=out_shape,
        grid=(m // tm, nk),
        in_specs=in_specs,
        out_specs=out_specs,
        scratch_shapes=[pltpu.VMEM((tm, LANE), F32)],
        compiler_params=_params("parallel", "arbitrary"),
        name="matmul_resid_norm",
    )(*operands)
    return (outs[0], outs[1]) if with_next else (outs[0], None)


def _ab_block(go, gi, val, ga, gg, ca_halo, cb_halo, aw_ref, bw_ref, bias, lng, lnb,
              exta_ref, extb_ref, z_ref):
    r, c = go.shape
    ca = gi * val
    exta_ref[0:HALO_A, :] = ca_halo
    exta_ref[HALO_A:HALO_A + r, :] = ca
    conv_a = aw_ref[CONV_A - 1:CONV_A, :] * ca
    for j in range(CONV_A - 1):
        off = HALO_A - (CONV_A - 1) + j
        conv_a = conv_a + aw_ref[j:j + 1, :] * exta_ref[off:off + r, :]
    y_a = go * conv_a
    cb = ga * jax.nn.sigmoid(gg)
    extb_ref[0:HALO_B, :] = cb_halo
    extb_ref[HALO_B:HALO_B + r, :] = cb
    rc = min(r, 32)
    cc = min(c, 512)
    base = HALO_B - (CONV_B - 1)
    for r0 in range(0, r, rc):
        for c0 in range(0, c, cc):
            acc = jnp.broadcast_to(bias[:, c0:c0 + cc], (rc, cc))
            for j in range(CONV_B):
                acc = acc + bw_ref[j:j + 1, c0:c0 + cc] * extb_ref[base + j + r0:base + j + r0 + rc, c0:c0 + cc]
            z_ref[r0:r0 + rc, c0:c0 + cc] = acc
    y_b = _silu(_layer_norm(z_ref[0:r, :], lng, lnb))
    return y_a, y_b, ca, cb


def _ab_prompt_kernel(go_ref, gi_ref, val_ref, ga_ref, gg_ref, gih_ref, valh_ref, gah_ref, ggh_ref,
                      aw_ref, bw_ref, bias_ref, lng_ref, lnb_ref,
                      y_ref, ca_tail_ref, cb_tail_ref, exta_ref, extb_ref, z_ref, *, n_i):
    i = pl.program_id(1)
    r, c = go_ref.shape
    first = i == 0
    ca_halo = jnp.where(first, 0.0, gih_ref[...] * valh_ref[...])
    cb_halo = jnp.where(first, 0.0, gah_ref[...] * jax.nn.sigmoid(ggh_ref[...]))
    y_a, y_b, ca, cb = _ab_block(go_ref[...], gi_ref[...], val_ref[...], ga_ref[...], gg_ref[...],
                                 ca_halo, cb_halo, aw_ref, bw_ref, bias_ref[...], lng_ref[...], lnb_ref[...],
                                 exta_ref, extb_ref, z_ref)
    y_ref[:, 0:c] = y_a.astype(BF16)
    y_ref[:, c:2 * c] = y_b.astype(BF16)

    @pl.when(i == n_i - 1)
    def _():
        ca_tail_ref[0] = exta_ref[r:r + HALO_A, :]
        cb_tail_ref[0] = extb_ref[r:r + HALO_B, :]


def mixer_ab_prompt(secs, n_b, t, a_w, b_w, b_bias, ln_g, ln_b):
    go, gi, val, ga, gg = secs
    c = go.shape[1]
    tr = _tile(t, 256, HALO_B)
    n_i = t // tr
    cur = pl.BlockSpec((tr, c), lambda b, i: (b * n_i + i, 0))

    def halo(rows):
        per = tr // rows
        return pl.BlockSpec((rows, c), lambda b, i: (jnp.maximum((b * n_i + i) * per - 1, 0), 0))

    def vec(rows):
        return pl.BlockSpec((rows, c), lambda b, i: (0, 0))

    kern = functools.partial(_ab_prompt_kernel, n_i=n_i)
    return pl.pallas_call(
        kern,
        out_shape=[jax.ShapeDtypeStruct((n_b * t, 2 * c), BF16),
                   jax.ShapeDtypeStruct((n_b, HALO_A, c), F32),
                   jax.ShapeDtypeStruct((n_b, HALO_B, c), F32)],
        grid=(n_b, n_i),
        in_specs=[cur, cur, cur, cur, cur, halo(HALO_A), halo(HALO_A), halo(HALO_B), halo(HALO_B),
                  vec(CONV_A), vec(CONV_B), vec(1), vec(1), vec(1)],
        out_specs=[pl.BlockSpec((tr, 2 * c), lambda b, i: (b * n_i + i, 0)),
                   pl.BlockSpec((1, HALO_A, c), lambda b, i: (b, 0, 0)),
                   pl.BlockSpec((1, HALO_B, c), lambda b, i: (b, 0, 0))],
        scratch_shapes=[pltpu.VMEM((HALO_A + tr, c), F32), pltpu.VMEM((HALO_B + tr, c), F32),
                        pltpu.VMEM((tr, c), F32)],
        compiler_params=_params("parallel", "arbitrary"),
        name="mixer_ab_prompt",
    )(go, gi, val, ga, gg, gi, val, ga, gg, a_w, b_w, b_bias.reshape(1, c), ln_g.reshape(1, c), ln_b.reshape(1, c))


def _ab_sample_kernel(go_ref, gi_ref, val_ref, ga_ref, gg_ref, ha_ref, hb_ref,
                      aw_ref, bw_ref, bias_ref, lng_ref, lnb_ref,
                      y_ref, ca_tail_ref, cb_tail_ref, exta_ref, extb_ref, z_ref, *, s):
    nb = ha_ref.shape[0]
    c = go_ref.shape[1]
    def one_stream(q, carry):
        rows = pl.ds(pl.multiple_of(q * s, s), s)
        y_a, y_b, ca, cb = _ab_block(go_ref[rows, :], gi_ref[rows, :], val_ref[rows, :], ga_ref[rows, :],
                                     gg_ref[rows, :], ha_ref[q], hb_ref[q], aw_ref, bw_ref, bias_ref[...],
                                     lng_ref[...], lnb_ref[...], exta_ref, extb_ref, z_ref)
        y_ref[rows, 0:c] = y_a.astype(BF16)
        y_ref[rows, c:2 * c] = y_b.astype(BF16)
        ca_tail_ref[q] = exta_ref[s:s + HALO_A, :]
        cb_tail_ref[q] = extb_ref[s:s + HALO_B, :]
        return carry

    lax.fori_loop(0, nb, one_stream, 0)


def mixer_ab_sample(secs, row0, n_b, s, hist_a, hist_b, a_w, b_w, b_bias, ln_g, ln_b):
    go, gi, val, ga, gg = secs
    c = go.shape[1]
    assert s >= HALO_B
    nb = _tile(n_b, 8, 1)
    assert row0 % (nb * s) == 0
    blk0 = row0 // (nb * s)
    ha = jnp.pad(hist_a, ((0, 0), (HALO_A - (CONV_A - 1), 0), (0, 0)))
    hb = jnp.pad(hist_b, ((0, 0), (HALO_B - (CONV_B - 1), 0), (0, 0)))
    cur = pl.BlockSpec((nb * s, c), lambda i: (blk0 + i, 0))

    def vec(rows):
        return pl.BlockSpec((rows, c), lambda i: (0, 0))

    kern = functools.partial(_ab_sample_kernel, s=s)
    return pl.pallas_call(
        kern,
        out_shape=[jax.ShapeDtypeStruct((n_b * s, 2 * c), BF16),
                   jax.ShapeDtypeStruct((n_b, HALO_A, c), F32),
                   jax.ShapeDtypeStruct((n_b, HALO_B, c), F32)],
        grid=(n_b // nb,),
        in_specs=[cur, cur, cur, cur, cur,
                  pl.BlockSpec((nb, HALO_A, c), lambda i: (i, 0, 0)),
                  pl.BlockSpec((nb, HALO_B, c), lambda i: (i, 0, 0)),
                  vec(CONV_A), vec(CONV_B), vec(1), vec(1), vec(1)],
        out_specs=[pl.BlockSpec((nb * s, 2 * c), lambda i: (i, 0)),
                   pl.BlockSpec((nb, HALO_A, c), lambda i: (i, 0, 0)),
                   pl.BlockSpec((nb, HALO_B, c), lambda i: (i, 0, 0))],
        scratch_shapes=[pltpu.VMEM((HALO_A + s, c), F32), pltpu.VMEM((HALO_B + s, c), F32),
                        pltpu.VMEM((s, c), F32)],
        compiler_params=_params("parallel"),
        name="mixer_ab_sample",
    )(go, gi, val, ga, gg, ha, hb, a_w, b_w, b_bias.reshape(1, c), ln_g.reshape(1, c), ln_b.reshape(1, c))


def _rope_tables(pos):
    half = HEAD_DIM // 2
    inv_freq = ROPE_THETA ** (-jnp.arange(half, dtype=F32) / half)
    ang = pos.astype(F32)[:, None] * inv_freq[None, :]
    cos = jnp.cos(ang)
    sin = jnp.sin(ang)
    return jnp.concatenate([cos, cos], axis=-1), jnp.concatenate([-sin, sin], axis=-1)


def _rope(x, cos2, sin2):
    return x * cos2 + pltpu.roll(x, HEAD_DIM // 2, 1) * sin2


def _sink_softmax_pv(s, sink, vv):
    m = jnp.maximum(jnp.max(s, axis=-1, keepdims=True), sink)
    e = jnp.exp(s - m)
    p = e / (jnp.sum(e, axis=-1, keepdims=True) + jnp.exp(sink - m))
    return jnp.dot(p.astype(BF16), vv, preferred_element_type=F32)


def _qk(q, kk):
    return lax.dot_general(q, kk, (((1,), (1,)), ((), ())), preferred_element_type=F32) * (HEAD_DIM ** -0.5)


def _attn_prompt_kernel(sink_ref, q_ref, k_ref, v_ref, kh_ref, vh_ref, cos_ref, sin_ref, cosh_ref, sinh_ref,
                        att_ref, krot_ref):
    i = pl.program_id(1)
    tq = q_ref.shape[0]
    cos, sin = cos_ref[...], sin_ref[...]
    cosh, sinh = cosh_ref[...], sinh_ref[...]
    n_keys = WINDOW + tq
    q_chunk = lax.broadcasted_iota(jnp.int32, (tq, n_keys), 0) // CHUNK
    k_col = lax.broadcasted_iota(jnp.int32, (tq, n_keys), 1)
    k_chunk = k_col // CHUNK
    n_back = WINDOW // CHUNK
    mask = (k_chunk >= q_chunk) & (k_chunk <= q_chunk + n_back)
    mask = mask & (k_col >= jnp.where(i > 0, 0, WINDOW))
    for h in range(N_KV_HEADS):
        hs = slice(h * HEAD_DIM, (h + 1) * HEAD_DIM)
        k_cur = _rope(k_ref[:, hs], cos, sin)
        krot_ref[:, hs] = k_cur
        kk = jnp.concatenate([_rope(kh_ref[:, hs], cosh, sinh), k_cur], axis=0).astype(BF16)
        vv = jnp.concatenate([vh_ref[:, hs], v_ref[:, hs]], axis=0).astype(BF16)
        for g in range(GQA_GROUP):
            n = h * GQA_GROUP + g
            ns = slice(n * HEAD_DIM, (n + 1) * HEAD_DIM)
            qn = _rope(q_ref[:, ns], cos, sin).astype(BF16)
            s = jnp.where(mask, _qk(qn, kk), -jnp.inf)
            att_ref[:, ns] = _sink_softmax_pv(s, sink_ref[n], vv).astype(BF16)


def attn_prompt(q, k, v, sink, n_b, t):
    qw, kw = q.shape[1], k.shape[1]
    tq = _tile(t, 256, WINDOW)
    n_i = t // tq
    per = tq // WINDOW
    cos2, sin2 = _rope_tables(jnp.arange(t, dtype=jnp.int32))
    cur = lambda w: pl.BlockSpec((tq, w), lambda b, i: (b * n_i + i, 0))
    halo = pl.BlockSpec((WINDOW, kw), lambda b, i: (jnp.maximum((b * n_i + i) * per - 1, 0), 0))
    tab = pl.BlockSpec((tq, HEAD_DIM), lambda b, i: (i, 0))
    tab_h = pl.BlockSpec((WINDOW, HEAD_DIM), lambda b, i: (jnp.maximum(i * per - 1, 0), 0))
    return pl.pallas_call(
        _attn_prompt_kernel,
        out_shape=[jax.ShapeDtypeStruct((n_b * t, qw), BF16), jax.ShapeDtypeStruct((n_b * t, kw), F32)],
        grid=(n_b, n_i),
        in_specs=[pl.BlockSpec(memory_space=pltpu.SMEM), cur(qw), cur(kw), cur(kw), halo, halo, tab, tab, tab_h, tab_h],
        out_specs=[cur(qw), cur(kw)],
        compiler_params=_params("parallel", "parallel"),
        name="attn_prompt",
    )(sink.reshape(-1), q, k, v, k, v, cos2, sin2, cos2, sin2)


def _attn_sample_kernel(sink_ref, q_ref, k_ref, v_ref, wk_ref, wv_ref, cos_ref, sin_ref, att_ref, krot_ref, *, s):
    nb = wk_ref.shape[0]
    cos, sin = cos_ref[...], sin_ref[...]
    def one_stream(b, carry):
        rows = pl.ds(pl.multiple_of(b * s, s), s)
        for h in range(N_KV_HEADS):
            hs = slice(h * HEAD_DIM, (h + 1) * HEAD_DIM)
            k_cur = _rope(k_ref[rows, hs], cos, sin)
            krot_ref[rows, hs] = k_cur
            kk = jnp.concatenate([wk_ref[b, :, hs], k_cur], axis=0).astype(BF16)
            vv = jnp.concatenate([wv_ref[b, :, hs], v_ref[rows, hs]], axis=0).astype(BF16)
            for g in range(GQA_GROUP):
                n = h * GQA_GROUP + g
                ns = slice(n * HEAD_DIM, (n + 1) * HEAD_DIM)
                qn = _rope(q_ref[rows, ns], cos, sin).astype(BF16)
                att_ref[rows, ns] = _sink_softmax_pv(_qk(qn, kk), sink_ref[n], vv).astype(BF16)
        return carry

    lax.fori_loop(0, nb, one_stream, 0)


def attn_sample(q, k, v, row0, n_b, s, win_k, win_v, sink):
    qw, kw = q.shape[1], k.shape[1]
    nb = _tile(n_b, 8, 1)
    assert row0 % (nb * s) == 0
    blk0 = row0 // (nb * s)
    cos2, sin2 = _rope_tables(PAST_LEN + jnp.arange(s, dtype=jnp.int32))
    cur = lambda w: pl.BlockSpec((nb * s, w), lambda i: (blk0 + i, 0))
    out = lambda w: pl.BlockSpec((nb * s, w), lambda i: (i, 0))
    win = pl.BlockSpec((nb, WINDOW, kw), lambda i: (i, 0, 0))
    tab = pl.BlockSpec((s, HEAD_DIM), lambda i: (0, 0))
    kern = functools.partial(_attn_sample_kernel, s=s)
    return pl.pallas_call(
        kern,
        out_shape=[jax.ShapeDtypeStruct((n_b * s, qw), BF16), jax.ShapeDtypeStruct((n_b * s, kw), F32)],
        grid=(n_b // nb,),
        in_specs=[pl.BlockSpec(memory_space=pltpu.SMEM), cur(qw), cur(kw), cur(kw), win, win, tab, tab],
        out_specs=[out(qw), out(kw)],
        compiler_params=_params("parallel"),
        name="attn_sample",
    )(sink.reshape(-1), q, k, v, win_k.reshape(n_b, WINDOW, kw), win_v.reshape(n_b, WINDOW, kw), cos2, sin2)


def _gmlp_kernel(u_ref, vg_ref, lng_ref, lnb_ref, ws_ref, bst_ref, y_ref, *rest, with_vn):
    tr, c = u_ref.shape
    r = ws_ref.shape[1]
    gd = c // GMLP_GROUPS
    vn = _layer_norm(vg_ref[...], lng_ref[...], lnb_ref[...])
    if with_vn:
        rest[0][...] = vn
    vnb = vn.astype(BF16)
    row_chunk = lax.broadcasted_iota(jnp.int32, (r, r), 0) // CHUNK
    col_chunk = lax.broadcasted_iota(jnp.int32, (r, r), 1) // CHUNK
    allowed = col_chunk <= row_chunk
    for g in range(GMLP_GROUPS):
        wg = jnp.where(allowed, ws_ref[g], 0.0).astype(BF16)
        bias = bst_ref[:, g:g + 1]
        cs = slice(g * gd, (g + 1) * gd)
        for r0 in range(0, tr, r):
            sg = jnp.dot(wg, vnb[r0:r0 + r, cs], preferred_element_type=F32) + bias
            y_ref[r0:r0 + r, cs] = (u_ref[r0:r0 + r, cs] * sg).astype(BF16)


def gmlp(u, vg, row0, n_rows, r, ln_g, ln_b, w_s, b_s, with_vn):
    c = u.shape[1]
    tr = _tile(n_rows, 256, r)
    assert row0 % tr == 0
    blk0 = row0 // tr
    ws = w_s[:, :r, :r]
    bst = b_s[:, :r].T
    cur = pl.BlockSpec((tr, c), lambda i: (blk0 + i, 0))
    out = pl.BlockSpec((tr, c), lambda i: (i, 0))
    vec = pl.BlockSpec((1, c), lambda i: (0, 0))
    out_shape = [jax.ShapeDtypeStruct((n_rows, c), BF16)]
    out_specs = [out]
    if with_vn:
        out_shape.append(jax.ShapeDtypeStruct((n_rows, c), F32))
        out_specs.append(out)
    kern = functools.partial(_gmlp_kernel, with_vn=with_vn)
    outs = pl.pallas_call(
        kern,
        out_shape=out_shape,
        grid=(n_rows // tr,),
        in_specs=[cur, cur, vec, vec,
                  pl.BlockSpec((GMLP_GROUPS, r, r), lambda i: (0, 0, 0)),
                  pl.BlockSpec((r, GMLP_GROUPS), lambda i: (0, 0))],
        out_specs=out_specs,
        compiler_params=_params("parallel"),
        name="gmlp",
    )(u, vg, ln_g.reshape(1, c), ln_b.reshape(1, c), ws, bst)
    return (outs[0], outs[1]) if with_vn else (outs[0], None)


def _xattn_kernel(q_ref, mk_ref, mv_ref, o_ref, *, tq):
    nb = mk_ref.shape[0]
    for b in range(nb):
        rows = slice(b * tq, (b + 1) * tq)
        for h in range(N_X_HEADS):
            hs = slice(h * HEAD_DIM, (h + 1) * HEAD_DIM)
            s = _qk(q_ref[rows, hs].astype(BF16), mk_ref[b, :, hs].astype(BF16))
            m = jnp.max(s, axis=-1, keepdims=True)
            e = jnp.exp(s - m)
            p = e / jnp.sum(e, axis=-1, keepdims=True)
            o = jnp.dot(p.astype(BF16), mv_ref[b, :, hs].astype(BF16), preferred_element_type=F32)
            o_ref[rows, hs] = o.astype(BF16)


def cross_attn(q, row0, n_b, t, mk, mv):
    w = q.shape[1]
    n_mem = mk.shape[1]
    if t >= 512:
        nb, tq = 1, _tile(t, 512, SUBLANE)
    else:
        nb, tq = _tile(n_b, 8, 1), t
    n_i = t // tq
    assert row0 % (nb * tq) == 0
    blk0 = row0 // (nb * tq)
    kern = functools.partial(_xattn_kernel, tq=tq)
    mem = pl.BlockSpec((nb, n_mem, w), lambda b, i: (b, 0, 0))
    return pl.pallas_call(
        kern,
        out_shape=jax.ShapeDtypeStruct((n_b * t, w), BF16),
        grid=(n_b // nb, n_i),
        in_specs=[pl.BlockSpec((nb * tq, w), lambda b, i: (blk0 + b * n_i + i, 0)), mem, mem],
        out_specs=pl.BlockSpec((nb * tq, w), lambda b, i: (b * n_i + i, 0)),
        compiler_params=_params("parallel", "parallel"),
        name="cross_attn",
    )(q, mk, mv)


def _stack3(w):
    return w.reshape((-1,) + w.shape[-2:])


def kernel(x_prompt, x_sample, mem_prompt, state_conv_a, state_conv_b, cache_win_k, cache_win_v, cache_mem_k, cache_mem_v, norms, ffn_w_gate, ffn_w_up, ffn_w_down, xattn_wq, xattn_wk, xattn_wv, xattn_wo, ab_w_in, a_conv_w, b_conv_w, b_conv_bias, b_ln_g, b_ln_b, ab_w_out, cd_w_in, c_sink, d_ln_g, d_ln_b, d_w_s, d_b_s, cd_w_out):
    n_bp, t, d = x_prompt.shape
    n_bs, s, _ = x_sample.shape
    depth = norms.shape[0]
    mp, ms = n_bp * t, n_bs * s
    n_mem = mem_prompt.shape[1]
    wx = xattn_wq.shape[2]

    x, h = rms_cast_join(x_prompt.reshape(mp, d), x_sample.reshape(ms, d), norms[0, 0])
    mem = mem_prompt.reshape(n_bp * n_mem, d)

    wg_all, wu_all = _stack3(ffn_w_gate), _stack3(ffn_w_up)
    wd_all = _stack3(ffn_w_down).astype(BF16)
    ab_out, cd_out, wo_all = ab_w_out.astype(BF16), cd_w_out.astype(BF16), xattn_wo.astype(BF16)

    def half_ffn(h, x, layer, half, g_post, g_next):
        e = 2 * layer + half
        a = swiglu_up(h, wg_all, wu_all, e)
        return matmul_resid_norm(a, wd_all, e, x, g_post, g_next, 0.5)

    conv_a_p, conv_a_s, conv_b_p, conv_b_s = [], [], [], []
    win_k_p, win_v_p, win_k_s, win_v_s, chunk_v_s = [], [], [], [], []
    mem_k_p, mem_v_p = [], []
    for layer in range(depth):
        g = norms[layer]
        x, h = half_ffn(h, x, layer, 0, g[1], g[2])

        if layer % 2 == 0:
            e = layer // 2
            c = a_conv_w.shape[2]
            secs = [matmul_cols(h, ab_w_in, e, j * c, c) for j in range(5)]
            conv = (a_conv_w[e], b_conv_w[e], b_conv_bias[e], b_ln_g[e], b_ln_b[e])
            y_p, ca_p, cb_p = mixer_ab_prompt(secs, n_bp, t, *conv)
            y_s, ca_s, cb_s = mixer_ab_sample(secs, mp, n_bs, s, state_conv_a[e], state_conv_b[e], *conv)
            conv_a_p.append(ca_p[:, HALO_A - (CONV_A - 1):])
            conv_a_s.append(ca_s[:, HALO_A - (CONV_A - 1):])
            conv_b_p.append(cb_p[:, HALO_B - (CONV_B - 1):])
            conv_b_s.append(cb_s[:, HALO_B - (CONV_B - 1):])
            y = jnp.concatenate([y_p, y_s], axis=0)
            w_out, e_out = ab_out, e
        else:
            o = layer // 2
            wd_ = d_ln_g.shape[1]
            kvw = N_KV_HEADS * HEAD_DIM
            qw = cd_w_in.shape[2] - 2 * kvw - 2 * wd_
            q = matmul_cols(h, cd_w_in, o, 0, qw)
            k = matmul_cols(h, cd_w_in, o, qw, kvw)
            v = matmul_cols(h, cd_w_in, o, qw + kvw, kvw)
            u = matmul_cols(h, cd_w_in, o, qw + 2 * kvw, wd_)
            vg = matmul_cols(h, cd_w_in, o, qw + 2 * kvw + wd_, wd_)
            att_p, krot_p = attn_prompt(q, k, v, c_sink[o], n_bp, t)
            att_s, krot_s = attn_sample(q, k, v, mp, n_bs, s, cache_win_k[o], cache_win_v[o], c_sink[o])
            yd_p, _ = gmlp(u, vg, 0, mp, GMLP_CHUNK, d_ln_g[o], d_ln_b[o], d_w_s[o], d_b_s[o], False)
            yd_s, vn_s = gmlp(u, vg, mp, ms, s, d_ln_g[o], d_ln_b[o], d_w_s[o], d_b_s[o], True)
            kv_shape = (N_KV_HEADS, HEAD_DIM)
            win_k_p.append(krot_p.reshape(n_bp, t, *kv_shape)[:, -WINDOW:])
            win_v_p.append(v[:mp].reshape(n_bp, t, *kv_shape)[:, -WINDOW:])
            k_all = jnp.concatenate([cache_win_k[o], krot_s.reshape(n_bs, s, *kv_shape)], axis=1)
            v_all = jnp.concatenate([cache_win_v[o], v[mp:].reshape(n_bs, s, *kv_shape)], axis=1)
            win_k_s.append(k_all[:, -WINDOW:])
            win_v_s.append(v_all[:, -WINDOW:])
            chunk_v_s.append(vn_s.reshape(n_bs, s, wd_))
            y = jnp.concatenate([jnp.concatenate([att_p, yd_p], axis=1),
                                 jnp.concatenate([att_s, yd_s], axis=1)], axis=0)
            w_out, e_out = cd_out, o
        x, h = matmul_resid_norm(y, w_out, e_out, x, g[3], g[4], 1.0)

        m_n = rms_cast(mem, g[8])
        mk = matmul_cols(m_n, xattn_wk, layer, 0, wx)
        mv = matmul_cols(m_n, xattn_wv, layer, 0, wx)
        mem_k_p.append(mk.reshape(n_bp, n_mem, N_X_HEADS, HEAD_DIM))
        mem_v_p.append(mv.reshape(n_bp, n_mem, N_X_HEADS, HEAD_DIM))
        qx = matmul_cols(h, xattn_wq, layer, 0, wx)
        o_p = cross_attn(qx, 0, n_bp, t, mk.reshape(n_bp, n_mem, wx), mv.reshape(n_bp, n_mem, wx))
        o_s = cross_attn(qx, mp, n_bs, s, cache_mem_k[layer].reshape(n_bs, n_mem, wx),
                         cache_mem_v[layer].reshape(n_bs, n_mem, wx))
        x, h = matmul_resid_norm(jnp.concatenate([o_p, o_s], axis=0), wo_all, layer, x, g[5], g[6], 1.0)

        g_next = norms[layer + 1, 0] if layer + 1 < depth else None
        x, h = half_ffn(h, x, layer, 1, g[7], g_next)

    return (x[:mp].reshape(n_bp, t, d), x[mp:].reshape(n_bs, s, d),
            jnp.stack(conv_a_p), jnp.stack(conv_a_s), jnp.stack(conv_b_p), jnp.stack(conv_b_s),
            jnp.stack(win_k_p), jnp.stack(win_v_p), jnp.stack(win_k_s), jnp.stack(win_v_s),
            jnp.stack(chunk_v_s), jnp.stack(mem_k_p), jnp.stack(mem_v_p))
```

```python
import functools
import itertools
import math

import jax
import jax.numpy as jnp
from jax import lax
from jax.experimental import pallas as pl
from jax.experimental.pallas import tpu as pltpu

EPS = 1e-6
CHUNK = 64
HEAD_DIM = 128
WINDOW = 128
ROPE_THETA = 10000.0
GMLP_CHUNK = 128
GMLP_GROUPS = 8
N_KV_HEADS = 4
GQA_GROUP = 4
N_X_HEADS = 4
CONV_A = 3
CONV_B = 31
PAST_LEN = 2048
N_NORMS = 9

V7X_VMEM_BYTES = 64 * 1024 * 1024
VMEM_LIMIT = V7X_VMEM_BYTES - 8 * 1024 * 1024
SUBLANE = 8
LANE = 128
HALO_A = SUBLANE
HALO_B = 4 * SUBLANE

BF16 = jnp.bfloat16
F32 = jnp.float32


def _tile(n, target, mult):
    t = min(n, target)
    t -= t % mult
    while t >= mult:
        if n % t == 0:
            return t
        t -= mult
    return n


def _params(*sem):
    return pltpu.CompilerParams(dimension_semantics=sem, vmem_limit_bytes=VMEM_LIMIT)


def _rms(x, g):
    return x * lax.rsqrt(jnp.mean(x * x, axis=-1, keepdims=True) + EPS) * g


def _layer_norm(x, g, b):
    mu = jnp.mean(x, axis=-1, keepdims=True)
    xc = x - mu
    var = jnp.mean(xc * xc, axis=-1, keepdims=True)
    return xc * lax.rsqrt(var + EPS) * g + b


def _silu(x):
    return x * jax.nn.sigmoid(x)


def _rms_cast2_kernel(xp_ref, xs_ref, g_ref, x_ref, h_ref, *, n_p):
    i = pl.program_id(0)

    @pl.when(i < n_p)
    def _():
        x = xp_ref[...]
        x_ref[...] = x
        h_ref[...] = _rms(x, g_ref[...]).astype(BF16)

    @pl.when(i >= n_p)
    def _():
        x = xs_ref[...]
        x_ref[...] = x
        h_ref[...] = _rms(x, g_ref[...]).astype(BF16)


def rms_cast_join(xp, xs, g):
    (mp, d), ms = xp.shape, xs.shape[0]
    tm = _tile(ms, 256, SUBLANE)
    assert mp % tm == 0
    n_p, n_s = mp // tm, ms // tm
    kern = functools.partial(_rms_cast2_kernel, n_p=n_p)
    row = lambda i: (i, 0)
    return pl.pallas_call(
        kern,
        out_shape=[jax.ShapeDtypeStruct((mp + ms, d), F32), jax.ShapeDtypeStruct((mp + ms, d), BF16)],
        grid=(n_p + n_s,),
        in_specs=[pl.BlockSpec((tm, d), lambda i: (jnp.minimum(i, n_p - 1), 0)),
                  pl.BlockSpec((tm, d), lambda i: (jnp.maximum(i - n_p, 0), 0)),
                  pl.BlockSpec((1, d), lambda i: (0, 0))],
        out_specs=[pl.BlockSpec((tm, d), row), pl.BlockSpec((tm, d), row)],
        compiler_params=_params("parallel"),
        name="rms_cast_join",
    )(xp, xs, g.reshape(1, d))


def _rms_cast_kernel(x_ref, g_ref, o_ref):
    o_ref[...] = _rms(x_ref[...], g_ref[...]).astype(o_ref.dtype)


def rms_cast(x, g):
    m, d = x.shape
    tm = _tile(m, 256, SUBLANE)
    return pl.pallas_call(
        _rms_cast_kernel,
        out_shape=jax.ShapeDtypeStruct((m, d), BF16),
        grid=(m // tm,),
        in_specs=[pl.BlockSpec((tm, d), lambda i: (i, 0)),
                  pl.BlockSpec((1, d), lambda i: (0, 0))],
        out_specs=pl.BlockSpec((tm, d), lambda i: (i, 0)),
        compiler_params=_params("parallel"),
        name="rms_cast",
    )(x, g.reshape(1, d))


ACT_ROWS = 2304


def _act_spec(tm, k):
    return pl.BlockSpec((tm, k), lambda i, j: (i, 0), pipeline_mode=pl.Buffered(1))


def _mm_kernel(a_ref, w_ref, o_ref):
    w = w_ref[...].astype(BF16)
    o_ref[...] = jnp.dot(a_ref[...], w, preferred_element_type=F32).astype(o_ref.dtype)


def matmul_cols(a, w, e, col0, ncols, out_dtype=F32):
    m, k = a.shape
    tm = _tile(m, ACT_ROWS, SUBLANE)
    tn = _tile(ncols, 512, LANE)
    assert col0 % tn == 0
    cb = col0 // tn
    return pl.pallas_call(
        _mm_kernel,
        out_shape=jax.ShapeDtypeStruct((m, ncols), out_dtype),
        grid=(m // tm, ncols // tn),
        in_specs=[_act_spec(tm, k),
                  pl.BlockSpec((None, k, tn), lambda i, j: (e, 0, cb + j))],
        out_specs=pl.BlockSpec((tm, tn), lambda i, j: (i, j)),
        compiler_params=_params("parallel", "parallel"),
        name="matmul_cols",
    )(a, w)


def _swiglu_kernel(h_ref, wg_ref, wu_ref, o_ref):
    h = h_ref[...]
    g = jnp.dot(h, wg_ref[...].astype(BF16), preferred_element_type=F32)
    u = jnp.dot(h, wu_ref[...].astype(BF16), preferred_element_type=F32)
    o_ref[...] = (_silu(g) * u).astype(o_ref.dtype)


def swiglu_up(h, wg, wu, e):
    m, k = h.shape
    f = wg.shape[2]
    tm = _tile(m, ACT_ROWS, SUBLANE)
    tn = _tile(f, 256, LANE)
    wspec = pl.BlockSpec((None, k, tn), lambda i, j: (e, 0, j))
    return pl.pallas_call(
        _swiglu_kernel,
        out_shape=jax.ShapeDtypeStruct((m, f), BF16),
        grid=(m // tm, f // tn),
        in_specs=[_act_spec(tm, k), wspec, wspec],
        out_specs=pl.BlockSpec((tm, tn), lambda i, j: (i, j)),
        compiler_params=_params("parallel", "parallel"),
        name="swiglu_up",
    )(h, wg, wu)


EPILOGUE_ROWS = 2 * SUBLANE
EPILOGUE_UNROLL = 4


def _mm_resid_kernel(*refs, parts, n_main, has_tail, splits, n_chunk, with_next):
    a_refs, w_ref = refs[:len(parts)], refs[len(parts)]
    refs = refs[len(parts) + 1:]
    if has_tail:
        at_ref, wt_ref = refs[:2]
        refs = refs[2:]
    x_ref, gpost_ref, gnext_ref = refs[:3]
    xo_refs = refs[3:3 + len(splits)]
    ho_ref = refs[3 + len(splits)] if with_next else None
    inv_ref = refs[-1]
    i, k = pl.program_id(0), pl.program_id(1)
    tm, n = xo_refs[0].shape
    nk = n_main + int(has_tail)
    n_tiles = max(s[1] for s in splits)

    def in_range(idx, lo, hi, full):
        conds = ([idx >= lo] if lo > 0 else []) + ([idx < hi] if hi < full else [])
        return functools.reduce(lambda p, q: p & q, conds) if conds else None

    def when(*conds):
        conds = [c for c in conds if c is not None]
        return pl.when(functools.reduce(lambda p, q: p & q, conds)) if conds else (lambda f: f())

    def accumulate(a_blk, w_blk, xo_ref, first):
        a = a_blk[...]
        for c in range(0, n, n_chunk):
            part = jnp.dot(a, w_blk[:, c:c + n_chunk], preferred_element_type=F32)
            if first:
                xo_ref[:, c:c + n_chunk] = part
            else:
                xo_ref[:, c:c + n_chunk] += part

    def epilogue(xo_ref):
        lanes = [slice(j, j + LANE) for j in range(0, n, LANE)]

        def rows_of(c):
            if isinstance(c, int):
                return pl.ds(c * EPILOGUE_ROWS, EPILOGUE_ROWS)
            return pl.ds(pl.multiple_of(c * EPILOGUE_ROWS, EPILOGUE_ROWS), EPILOGUE_ROWS)

        def inv_rms(sq_lanes):
            ms = jnp.sum(sq_lanes, axis=-1, keepdims=True) * (1.0 / n)
            return jnp.broadcast_to(lax.rsqrt(ms + EPS), (EPILOGUE_ROWS, LANE))

        def stats_rows(c):
            rows = rows_of(c)
            sq = jnp.zeros((EPILOGUE_ROWS, LANE), F32)
            for ls in lanes:
                f = xo_ref[rows, ls]
                sq = sq + f * f
            return sq

        def resid_rows(c):
            rows = rows_of(c)
            r = inv_ref[rows, :]
            sq = jnp.zeros((EPILOGUE_ROWS, LANE), F32)
            for ls in lanes:
                xn = x_ref[rows, ls] + xo_ref[rows, ls] * r * gpost_ref[:, ls]
                xo_ref[rows, ls] = xn
                sq = sq + xn * xn
            return sq

        def next_step(c, carry):
            rows = rows_of(c)
            r = inv_ref[rows, :]
            for ls in lanes:
                ho_ref[rows, ls] = (xo_ref[rows, ls] * r * gnext_ref[:, ls]).astype(BF16)
            return carry

        steps = tm // EPILOGUE_ROWS
        u = EPILOGUE_UNROLL
        assert steps % u == 0

        def sweep(rows_fn, keep_stats):
            def batch(b):
                return tuple(rows_fn(b * u + j) for j in range(u))

            def finish(b, sqs):
                if keep_stats:
                    for j in range(u):
                        inv_ref[rows_of(b * u + j), :] = inv_rms(sqs[j])

            def step(b, sqs_prev):
                sqs = batch(b)
                finish(b - 1, sqs_prev)
                return sqs

            n_batches = steps // u
            finish(n_batches - 1, lax.fori_loop(1, n_batches, step, batch(0)))

        sweep(stats_rows, True)
        sweep(resid_rows, with_next)
        if with_next:
            lax.fori_loop(0, steps, next_step, 0, unroll=u)

    for (s0, s1), xo_ref in zip(splits, xo_refs):
        for (r0, r1, k0, k1), a_ref in zip(parts, a_refs):
            lo, hi = max(r0, s0), min(r1, s1)
            if lo >= hi:
                continue
            rows_ok = in_range(i, lo, hi, n_tiles)
            if k0 == 0:
                when(rows_ok, k == 0)(functools.partial(accumulate, a_ref, w_ref, xo_ref, True))
                if k1 > 1:
                    when(rows_ok, k > 0, in_range(k, 0, k1, nk))(
                        functools.partial(accumulate, a_ref, w_ref, xo_ref, False))
            else:
                when(rows_ok, in_range(k, k0, k1, nk))(functools.partial(accumulate, a_ref, w_ref, xo_ref, False))
        rows_ok = in_range(i, s0, s1, n_tiles)
        if has_tail:
            when(rows_ok, k == n_main)(functools.partial(accumulate, at_ref, wt_ref, xo_ref, False))
        when(rows_ok, k == nk - 1)(functools.partial(epilogue, xo_ref))


def matmul_resid_norm(a_parts, w, e, x, g_post, g_next, scale, out_rows=None):
    m, n = x.shape
    kdim = w.shape[1]
    edges = [row0 for _, row0, _ in a_parts] + list(itertools.accumulate(out_rows or []))
    tm = _tile(math.gcd(m, *edges), 512, EPILOGUE_ROWS)
    tk = min(512, kdim)
    n_main, tail = kdim // tk, kdim % tk
    has_tail = tail > 0
    nk = n_main + int(has_tail)
    with_next = g_next is not None
    assert not (with_next and out_rows is not None)
    gn = g_next if with_next else g_post
    gp = g_post if scale == 1.0 else g_post * scale
    row = lambda i, k: (i, 0)
    fixed = lambda i, k: (0, 0)

    def clipped(lo, hi):
        return lambda v: jnp.clip(v - lo, 0, hi - lo - 1)

    parts, in_specs, operands = [], [], []
    for arr, row0, col0 in a_parts:
        rows, cols = arr.shape
        assert row0 % tm == 0 and rows % tm == 0 and col0 % tk == 0
        assert cols % tk == 0 or (len(a_parts) == 1 and has_tail)
        part = (row0 // tm, (row0 + rows) // tm, col0 // tk, (col0 + cols) // tk)
        ri, ki = clipped(part[0], part[1]), clipped(part[2], part[3])
        parts.append(part)
        in_specs.append(pl.BlockSpec((tm, tk), lambda i, k, ri=ri, ki=ki: (ri(i), ki(k))))
        operands.append(arr)
    in_specs.append(pl.BlockSpec((None, tk, n), lambda i, k: (e, jnp.minimum(k, n_main - 1), 0)))
    operands.append(w)
    if has_tail:
        assert tail % LANE == 0 and (n_main * tk) % tail == 0
        tail_blk = (n_main * tk) // tail
        in_specs += [pl.BlockSpec((tm, tail), lambda i, k: (i, tail_blk)),
                     pl.BlockSpec((None, tail, n), lambda i, k: (e, tail_blk, 0))]
        operands += [a_parts[0][0], w]
    x_mode = {} if out_rows is None else dict(pipeline_mode=pl.Buffered(1))
    in_specs += [pl.BlockSpec((tm, n), row, **x_mode), pl.BlockSpec((1, n), fixed), pl.BlockSpec((1, n), fixed)]
    operands += [x, gp.reshape(1, n), gn.reshape(1, n)]

    splits, out_shape, out_specs, row0 = [], [], [], 0
    for rws in ([m] if out_rows is None else out_rows):
        split = (row0 // tm, (row0 + rws) // tm)
        ri = clipped(*split)
        splits.append(split)
        out_shape.append(jax.ShapeDtypeStruct((rws, n), F32))
        out_specs.append(pl.BlockSpec((tm, n), lambda i, k, ri=ri: (ri(i), 0)))
        row0 += rws
    if with_next:
        out_shape.append(jax.ShapeDtypeStruct((m, n), BF16))
        out_specs.append(pl.BlockSpec((tm, n), row))
    kern = functools.partial(_mm_resid_kernel, parts=tuple(parts), n_main=n_main, has_tail=has_tail,
                             splits=tuple(splits), n_chunk=_tile(n, 1024, LANE), with_next=with_next)
    outs = pl.pallas_call(
        kern,
        out_shape=out_shape,
        grid=(m // tm, nk),
        in_specs=in_specs,
        out_specs=out_specs,
        scratch_shapes=[pltpu.VMEM((tm, LANE), F32)],
        compiler_params=_params("arbitrary", "arbitrary"),
        name="matmul_resid_norm",
    )(*operands)
    if out_rows is not None:
        return tuple(outs), None
    return (outs[0], outs[1]) if with_next else (outs[0], None)


def _tap_rows(w):
    return jnp.repeat(w, SUBLANE, axis=0)


def _tap_weight(w_ref, j, cols, rows):
    tile = w_ref[j * SUBLANE:(j + 1) * SUBLANE, cols]
    return jnp.concatenate([tile] * (rows // SUBLANE), axis=0)


def _ab_block(go, gi, val, ga, gg, ca_halo, cb_halo, aw_ref, bw_ref, bias, lng, lnb,
              exta_ref, extb_ref, z_ref):
    r, c = go.shape
    ca = gi * val
    exta_ref[0:HALO_A, :] = ca_halo
    exta_ref[HALO_A:HALO_A + r, :] = ca
    conv_a = _tap_weight(aw_ref, CONV_A - 1, slice(None), r) * ca
    for j in range(CONV_A - 1):
        off = HALO_A - (CONV_A - 1) + j
        conv_a = conv_a + _tap_weight(aw_ref, j, slice(None), r) * exta_ref[off:off + r, :]
    y_a = go * conv_a
    cb = ga * jax.nn.sigmoid(gg)
    extb_ref[0:HALO_B, :] = cb_halo
    extb_ref[HALO_B:HALO_B + r, :] = cb
    rc = min(r, 64)
    cc = min(c, 256)
    base = HALO_B - (CONV_B - 1)
    phases = {}
    for j in range(CONV_B):
        phases.setdefault((base + j) % SUBLANE, []).append(j)
    for r0 in range(0, r, rc):
        for c0 in range(0, c, cc):
            cols = slice(c0, c0 + cc)
            acc = jnp.broadcast_to(bias[:, cols], (rc, cc))
            for b, taps in sorted(phases.items()):
                span = SUBLANE * max((base + j) // SUBLANE for j in taps) + rc
                win = extb_ref[r0 + b:r0 + b + span, cols]
                for j in taps:
                    a = SUBLANE * ((base + j) // SUBLANE)
                    acc = acc + _tap_weight(bw_ref, j, cols, rc) * win[a:a + rc]
            z_ref[r0:r0 + rc, cols] = acc
    y_b = _silu(_layer_norm(z_ref[0:r, :], lng, lnb))
    return y_a, y_b, ca, cb


def _ab_prompt_kernel(go_ref, gi_ref, val_ref, ga_ref, gg_ref, gih_ref, valh_ref, gah_ref, ggh_ref,
                      aw_ref, bw_ref, bias_ref, lng_ref, lnb_ref,
                      y_ref, ca_tail_ref, cb_tail_ref, exta_ref, extb_ref, z_ref, *, n_i):
    i = pl.program_id(1)
    r, c = go_ref.shape
    first = i == 0
    ca_halo = jnp.where(first, 0.0, gih_ref[...] * valh_ref[...])
    cb_halo = jnp.where(first, 0.0, gah_ref[...] * jax.nn.sigmoid(ggh_ref[...]))
    y_a, y_b, ca, cb = _ab_block(go_ref[...], gi_ref[...], val_ref[...], ga_ref[...], gg_ref[...],
                                 ca_halo, cb_halo, aw_ref, bw_ref, bias_ref[...], lng_ref[...], lnb_ref[...],
                                 exta_ref, extb_ref, z_ref)
    y_ref[:, 0:c] = y_a.astype(BF16)
    y_ref[:, c:2 * c] = y_b.astype(BF16)

    @pl.when(i == n_i - 1)
    def _():
        ca_tail_ref[0] = exta_ref[r:r + HALO_A, :]
        cb_tail_ref[0] = extb_ref[r:r + HALO_B, :]


def mixer_ab_prompt(secs, n_b, t, a_w, b_w, b_bias, ln_g, ln_b):
    go, gi, val, ga, gg = secs
    c = go.shape[1]
    tr = _tile(t, 256, HALO_B)
    n_i = t // tr
    cur = pl.BlockSpec((tr, c), lambda b, i: (b * n_i + i, 0))

    def halo(rows):
        per = tr // rows
        return pl.BlockSpec((rows, c), lambda b, i: (jnp.maximum((b * n_i + i) * per - 1, 0), 0))

    def vec(rows):
        return pl.BlockSpec((rows, c), lambda b, i: (0, 0))

    kern = functools.partial(_ab_prompt_kernel, n_i=n_i)
    return pl.pallas_call(
        kern,
        out_shape=[jax.ShapeDtypeStruct((n_b * t, 2 * c), BF16),
                   jax.ShapeDtypeStruct((n_b, HALO_A, c), F32),
                   jax.ShapeDtypeStruct((n_b, HALO_B, c), F32)],
        grid=(n_b, n_i),
        in_specs=[cur, cur, cur, cur, cur, halo(HALO_A), halo(HALO_A), halo(HALO_B), halo(HALO_B),
                  vec(CONV_A * SUBLANE), vec(CONV_B * SUBLANE), vec(1), vec(1), vec(1)],
        out_specs=[pl.BlockSpec((tr, 2 * c), lambda b, i: (b * n_i + i, 0)),
                   pl.BlockSpec((1, HALO_A, c), lambda b, i: (b, 0, 0)),
                   pl.BlockSpec((1, HALO_B, c), lambda b, i: (b, 0, 0))],
        scratch_shapes=[pltpu.VMEM((HALO_A + tr, c), F32), pltpu.VMEM((HALO_B + tr, c), F32),
                        pltpu.VMEM((tr, c), F32)],
        compiler_params=_params("parallel", "arbitrary"),
        name="mixer_ab_prompt",
    )(go, gi, val, ga, gg, gi, val, ga, gg, _tap_rows(a_w), _tap_rows(b_w),
      b_bias.reshape(1, c), ln_g.reshape(1, c), ln_b.reshape(1, c))


def _ab_sample_kernel(go_ref, gi_ref, val_ref, ga_ref, gg_ref, ha_ref, hb_ref,
                      aw_ref, bw_ref, bias_ref, lng_ref, lnb_ref,
                      y_ref, ca_tail_ref, cb_tail_ref, exta_ref, extb_ref, z_ref, *, s):
    nb = ha_ref.shape[0]
    c = go_ref.shape[1]
    def one_stream(q, carry):
        rows = pl.ds(pl.multiple_of(q * s, s), s)
        y_a, y_b, ca, cb = _ab_block(go_ref[rows, :], gi_ref[rows, :], val_ref[rows, :], ga_ref[rows, :],
                                     gg_ref[rows, :], ha_ref[q], hb_ref[q], aw_ref, bw_ref, bias_ref[...],
                                     lng_ref[...], lnb_ref[...], exta_ref, extb_ref, z_ref)
        y_ref[rows, 0:c] = y_a.astype(BF16)
        y_ref[rows, c:2 * c] = y_b.astype(BF16)
        ca_tail_ref[q] = exta_ref[s:s + HALO_A, :]
        cb_tail_ref[q] = extb_ref[s:s + HALO_B, :]
        return carry

    lax.fori_loop(0, nb, one_stream, 0)


def mixer_ab_sample(secs, row0, n_b, s, hist_a, hist_b, a_w, b_w, b_bias, ln_g, ln_b):
    go, gi, val, ga, gg = secs
    c = go.shape[1]
    assert s >= HALO_B
    nb = _tile(n_b, 8, 1)
    assert row0 % (nb * s) == 0
    blk0 = row0 // (nb * s)
    ha = jnp.pad(hist_a, ((0, 0), (HALO_A - (CONV_A - 1), 0), (0, 0)))
    hb = jnp.pad(hist_b, ((0, 0), (HALO_B - (CONV_B - 1), 0), (0, 0)))
    cur = pl.BlockSpec((nb * s, c), lambda i: (blk0 + i, 0))

    def vec(rows):
        return pl.BlockSpec((rows, c), lambda i: (0, 0))

    kern = functools.partial(_ab_sample_kernel, s=s)
    return pl.pallas_call(
        kern,
        out_shape=[jax.ShapeDtypeStruct((n_b * s, 2 * c), BF16),
                   jax.ShapeDtypeStruct((n_b, HALO_A, c), F32),
                   jax.ShapeDtypeStruct((n_b, HALO_B, c), F32)],
        grid=(n_b // nb,),
        in_specs=[cur, cur, cur, cur, cur,
                  pl.BlockSpec((nb, HALO_A, c), lambda i: (i, 0, 0)),
                  pl.BlockSpec((nb, HALO_B, c), lambda i: (i, 0, 0)),
                  vec(CONV_A * SUBLANE), vec(CONV_B * SUBLANE), vec(1), vec(1), vec(1)],
        out_specs=[pl.BlockSpec((nb * s, 2 * c), lambda i: (i, 0)),
                   pl.BlockSpec((nb, HALO_A, c), lambda i: (i, 0, 0)),
                   pl.BlockSpec((nb, HALO_B, c), lambda i: (i, 0, 0))],
        scratch_shapes=[pltpu.VMEM((HALO_A + s, c), F32), pltpu.VMEM((HALO_B + s, c), F32),
                        pltpu.VMEM((s, c), F32)],
        compiler_params=_params("parallel"),
        name="mixer_ab_sample",
    )(go, gi, val, ga, gg, ha, hb, _tap_rows(a_w), _tap_rows(b_w),
      b_bias.reshape(1, c), ln_g.reshape(1, c), ln_b.reshape(1, c))


def _rope_tables(pos):
    half = HEAD_DIM // 2
    inv_freq = ROPE_THETA ** (-jnp.arange(half, dtype=F32) / half)
    ang = pos.astype(F32)[:, None] * inv_freq[None, :]
    cos = jnp.cos(ang)
    sin = jnp.sin(ang)
    return jnp.concatenate([cos, cos], axis=-1), jnp.concatenate([-sin, sin], axis=-1)


def _rope(x, cos2, sin2):
    return x * cos2 + pltpu.roll(x, HEAD_DIM // 2, 1) * sin2


def _sink_softmax_pv(s, sink, vv):
    m = jnp.maximum(jnp.max(s, axis=-1, keepdims=True), sink)
    e = jnp.exp(s - m)
    p = e / (jnp.sum(e, axis=-1, keepdims=True) + jnp.exp(sink - m))
    return jnp.dot(p.astype(BF16), vv, preferred_element_type=F32)


def _qk(q, kk):
    return lax.dot_general(q, kk, (((1,), (1,)), ((), ())), preferred_element_type=F32) * (HEAD_DIM ** -0.5)


def _attn_prompt_kernel(sink_ref, q_ref, k_ref, v_ref, kh_ref, vh_ref, cos_ref, sin_ref, cosh_ref, sinh_ref,
                        att_ref, krot_ref):
    i = pl.program_id(1)
    tq = q_ref.shape[0]
    cos, sin = cos_ref[...], sin_ref[...]
    cosh, sinh = cosh_ref[...], sinh_ref[...]
    n_keys = WINDOW + tq
    q_chunk = lax.broadcasted_iota(jnp.int32, (tq, n_keys), 0) // CHUNK
    k_col = lax.broadcasted_iota(jnp.int32, (tq, n_keys), 1)
    k_chunk = k_col // CHUNK
    n_back = WINDOW // CHUNK
    mask = (k_chunk >= q_chunk) & (k_chunk <= q_chunk + n_back)
    mask = mask & (k_col >= jnp.where(i > 0, 0, WINDOW))
    for h in range(N_KV_HEADS):
        hs = slice(h * HEAD_DIM, (h + 1) * HEAD_DIM)
        k_cur = _rope(k_ref[:, hs], cos, sin)
        krot_ref[:, hs] = k_cur
        kk = jnp.concatenate([_rope(kh_ref[:, hs], cosh, sinh), k_cur], axis=0).astype(BF16)
        vv = jnp.concatenate([vh_ref[:, hs], v_ref[:, hs]], axis=0).astype(BF16)
        for g in range(GQA_GROUP):
            n = h * GQA_GROUP + g
            ns = slice(n * HEAD_DIM, (n + 1) * HEAD_DIM)
            qn = _rope(q_ref[:, ns], cos, sin).astype(BF16)
            s = jnp.where(mask, _qk(qn, kk), -jnp.inf)
            att_ref[:, ns] = _sink_softmax_pv(s, sink_ref[n], vv).astype(BF16)


def attn_prompt(q, k, v, sink, n_b, t):
    qw, kw = q.shape[1], k.shape[1]
    tq = _tile(t, 256, WINDOW)
    n_i = t // tq
    per = tq // WINDOW
    cos2, sin2 = _rope_tables(jnp.arange(t, dtype=jnp.int32))
    cur = lambda w: pl.BlockSpec((tq, w), lambda b, i: (b * n_i + i, 0))
    halo = pl.BlockSpec((WINDOW, kw), lambda b, i: (jnp.maximum((b * n_i + i) * per - 1, 0), 0))
    tab = pl.BlockSpec((tq, HEAD_DIM), lambda b, i: (i, 0))
    tab_h = pl.BlockSpec((WINDOW, HEAD_DIM), lambda b, i: (jnp.maximum(i * per - 1, 0), 0))
    return pl.pallas_call(
        _attn_prompt_kernel,
        out_shape=[jax.ShapeDtypeStruct((n_b * t, qw), BF16), jax.ShapeDtypeStruct((n_b * t, kw), F32)],
        grid=(n_b, n_i),
        in_specs=[pl.BlockSpec(memory_space=pltpu.SMEM), cur(qw), cur(kw), cur(kw), halo, halo, tab, tab, tab_h, tab_h],
        out_specs=[cur(qw), cur(kw)],
        compiler_params=_params("parallel", "parallel"),
        name="attn_prompt",
    )(sink.reshape(-1), q, k, v, k, v, cos2, sin2, cos2, sin2)


def _attn_sample_kernel(sink_ref, q_ref, k_ref, v_ref, wk_ref, wv_ref, cos_ref, sin_ref, att_ref, krot_ref, *, s):
    nb = wk_ref.shape[0]
    cos, sin = cos_ref[...], sin_ref[...]
    def one_stream(b, carry):
        rows = pl.ds(pl.multiple_of(b * s, s), s)
        for h in range(N_KV_HEADS):
            hs = slice(h * HEAD_DIM, (h + 1) * HEAD_DIM)
            k_cur = _rope(k_ref[rows, hs], cos, sin)
            krot_ref[rows, hs] = k_cur
            kk = jnp.concatenate([wk_ref[b, :, hs], k_cur], axis=0).astype(BF16)
            vv = jnp.concatenate([wv_ref[b, :, hs], v_ref[rows, hs]], axis=0).astype(BF16)
            for g in range(GQA_GROUP):
                n = h * GQA_GROUP + g
                ns = slice(n * HEAD_DIM, (n + 1) * HEAD_DIM)
                qn = _rope(q_ref[rows, ns], cos, sin).astype(BF16)
                att_ref[rows, ns] = _sink_softmax_pv(_qk(qn, kk), sink_ref[n], vv).astype(BF16)
        return carry

    lax.fori_loop(0, nb, one_stream, 0)


def attn_sample(q, k, v, row0, n_b, s, win_k, win_v, sink):
    qw, kw = q.shape[1], k.shape[1]
    nb = _tile(n_b, 8, 1)
    assert row0 % (nb * s) == 0
    blk0 = row0 // (nb * s)
    cos2, sin2 = _rope_tables(PAST_LEN + jnp.arange(s, dtype=jnp.int32))
    cur = lambda w: pl.BlockSpec((nb * s, w), lambda i: (blk0 + i, 0))
    out = lambda w: pl.BlockSpec((nb * s, w), lambda i: (i, 0))
    win = pl.BlockSpec((nb, WINDOW, kw), lambda i: (i, 0, 0))
    tab = pl.BlockSpec((s, HEAD_DIM), lambda i: (0, 0))
    kern = functools.partial(_attn_sample_kernel, s=s)
    return pl.pallas_call(
        kern,
        out_shape=[jax.ShapeDtypeStruct((n_b * s, qw), BF16), jax.ShapeDtypeStruct((n_b * s, kw), F32)],
        grid=(n_b // nb,),
        in_specs=[pl.BlockSpec(memory_space=pltpu.SMEM), cur(qw), cur(kw), cur(kw), win, win, tab, tab],
        out_specs=[out(qw), out(kw)],
        compiler_params=_params("parallel"),
        name="attn_sample",
    )(sink.reshape(-1), q, k, v, win_k.reshape(n_b, WINDOW, kw), win_v.reshape(n_b, WINDOW, kw), cos2, sin2)


def _gmlp_kernel(u_ref, vg_ref, lng_ref, lnb_ref, ws_ref, bst_ref, y_ref, *rest, with_vn):
    tr, c = u_ref.shape
    r = ws_ref.shape[1]
    gd = c // GMLP_GROUPS
    vn = _layer_norm(vg_ref[...], lng_ref[...], lnb_ref[...])
    if with_vn:
        rest[0][...] = vn
    vnb = vn.astype(BF16)
    row_chunk = lax.broadcasted_iota(jnp.int32, (r, r), 0) // CHUNK
    col_chunk = lax.broadcasted_iota(jnp.int32, (r, r), 1) // CHUNK
    allowed = col_chunk <= row_chunk
    for g in range(GMLP_GROUPS):
        wg = jnp.where(allowed, ws_ref[g], 0.0).astype(BF16)
        bias = bst_ref[:, g:g + 1]
        cs = slice(g * gd, (g + 1) * gd)
        for r0 in range(0, tr, r):
            sg = jnp.dot(wg, vnb[r0:r0 + r, cs], preferred_element_type=F32) + bias
            y_ref[r0:r0 + r, cs] = (u_ref[r0:r0 + r, cs] * sg).astype(BF16)


def gmlp(u, vg, row0, n_rows, r, ln_g, ln_b, w_s, b_s, with_vn):
    c = u.shape[1]
    tr = _tile(n_rows, 256, r)
    assert row0 % tr == 0
    blk0 = row0 // tr
    ws = w_s[:, :r, :r]
    bst = b_s[:, :r].T
    cur = pl.BlockSpec((tr, c), lambda i: (blk0 + i, 0))
    out = pl.BlockSpec((tr, c), lambda i: (i, 0))
    vec = pl.BlockSpec((1, c), lambda i: (0, 0))
    out_shape = [jax.ShapeDtypeStruct((n_rows, c), BF16)]
    out_specs = [out]
    if with_vn:
        out_shape.append(jax.ShapeDtypeStruct((n_rows, c), F32))
        out_specs.append(out)
    kern = functools.partial(_gmlp_kernel, with_vn=with_vn)
    outs = pl.pallas_call(
        kern,
        out_shape=out_shape,
        grid=(n_rows // tr,),
        in_specs=[cur, cur, vec, vec,
                  pl.BlockSpec((GMLP_GROUPS, r, r), lambda i: (0, 0, 0)),
                  pl.BlockSpec((r, GMLP_GROUPS), lambda i: (0, 0))],
        out_specs=out_specs,
        compiler_params=_params("parallel"),
        name="gmlp",
    )(u, vg, ln_g.reshape(1, c), ln_b.reshape(1, c), ws, bst)
    return (outs[0], outs[1]) if with_vn else (outs[0], None)


def _xattn_kernel(q_ref, mk_ref, mv_ref, o_ref, *, tq):
    nb = mk_ref.shape[0]
    for b in range(nb):
        rows = slice(b * tq, (b + 1) * tq)
        for h in range(N_X_HEADS):
            hs = slice(h * HEAD_DIM, (h + 1) * HEAD_DIM)
            s = _qk(q_ref[rows, hs].astype(BF16), mk_ref[b, :, hs].astype(BF16))
            m = jnp.max(s, axis=-1, keepdims=True)
            e = jnp.exp(s - m)
            p = e / jnp.sum(e, axis=-1, keepdims=True)
            o = jnp.dot(p.astype(BF16), mv_ref[b, :, hs].astype(BF16), preferred_element_type=F32)
            o_ref[rows, hs] = o.astype(BF16)


def cross_attn(q, row0, n_b, t, mk, mv):
    w = q.shape[1]
    n_mem = mk.shape[1]
    if t >= 512:
        nb, tq = 1, _tile(t, 512, SUBLANE)
    else:
        nb, tq = _tile(n_b, 8, 1), t
    n_i = t // tq
    assert row0 % (nb * tq) == 0
    blk0 = row0 // (nb * tq)
    kern = functools.partial(_xattn_kernel, tq=tq)
    mem = pl.BlockSpec((nb, n_mem, w), lambda b, i: (b, 0, 0))
    return pl.pallas_call(
        kern,
        out_shape=jax.ShapeDtypeStruct((n_b * t, w), BF16),
        grid=(n_b // nb, n_i),
        in_specs=[pl.BlockSpec((nb * tq, w), lambda b, i: (blk0 + b * n_i + i, 0)), mem, mem],
        out_specs=pl.BlockSpec((nb * tq, w), lambda b, i: (b * n_i + i, 0)),
        compiler_params=_params("parallel", "parallel"),
        name="cross_attn",
    )(q, mk, mv)


def _stack3(w):
    return w.reshape((-1,) + w.shape[-2:])


def kernel(x_prompt, x_sample, mem_prompt, state_conv_a, state_conv_b, cache_win_k, cache_win_v, cache_mem_k, cache_mem_v, norms, ffn_w_gate, ffn_w_up, ffn_w_down, xattn_wq, xattn_wk, xattn_wv, xattn_wo, ab_w_in, a_conv_w, b_conv_w, b_conv_bias, b_ln_g, b_ln_b, ab_w_out, cd_w_in, c_sink, d_ln_g, d_ln_b, d_w_s, d_b_s, cd_w_out):
    n_bp, t, d = x_prompt.shape
    n_bs, s, _ = x_sample.shape
    depth = norms.shape[0]
    mp, ms = n_bp * t, n_bs * s
    n_mem = mem_prompt.shape[1]
    wx = xattn_wq.shape[2]

    x, h = rms_cast_join(x_prompt.reshape(mp, d), x_sample.reshape(ms, d), norms[0, 0])
    mem = mem_prompt.reshape(n_bp * n_mem, d)

    wg_all, wu_all = _stack3(ffn_w_gate), _stack3(ffn_w_up)
    wd_all = _stack3(ffn_w_down).astype(BF16)
    ab_out, cd_out, wo_all = ab_w_out.astype(BF16), cd_w_out.astype(BF16), xattn_wo.astype(BF16)

    def half_ffn(h, x, layer, half, g_post, g_next, out_rows=None):
        e = 2 * layer + half
        a = swiglu_up(h, wg_all, wu_all, e)
        return matmul_resid_norm([(a, 0, 0)], wd_all, e, x, g_post, g_next, 0.5, out_rows)

    conv_a_p, conv_a_s, conv_b_p, conv_b_s = [], [], [], []
    win_k_p, win_v_p, win_k_s, win_v_s, chunk_v_s = [], [], [], [], []
    mem_k_p, mem_v_p = [], []
    for layer in range(depth):
        g = norms[layer]
        x, h = half_ffn(h, x, layer, 0, g[1], g[2])

        if layer % 2 == 0:
            e = layer // 2
            c = a_conv_w.shape[2]
            secs = [matmul_cols(h, ab_w_in, e, j * c, c) for j in range(5)]
            conv = (a_conv_w[e], b_conv_w[e], b_conv_bias[e], b_ln_g[e], b_ln_b[e])
            y_p, ca_p, cb_p = mixer_ab_prompt(secs, n_bp, t, *conv)
            y_s, ca_s, cb_s = mixer_ab_sample(secs, mp, n_bs, s, state_conv_a[e], state_conv_b[e], *conv)
            conv_a_p.append(ca_p[:, HALO_A - (CONV_A - 1):])
            conv_a_s.append(ca_s[:, HALO_A - (CONV_A - 1):])
            conv_b_p.append(cb_p[:, HALO_B - (CONV_B - 1):])
            conv_b_s.append(cb_s[:, HALO_B - (CONV_B - 1):])
            y_parts = [(y_p, 0, 0), (y_s, mp, 0)]
            w_out, e_out = ab_out, e
        else:
            o = layer // 2
            wd_ = d_ln_g.shape[1]
            kvw = N_KV_HEADS * HEAD_DIM
            qw = cd_w_in.shape[2] - 2 * kvw - 2 * wd_
            q = matmul_cols(h, cd_w_in, o, 0, qw)
            k = matmul_cols(h, cd_w_in, o, qw, kvw)
            v = matmul_cols(h, cd_w_in, o, qw + kvw, kvw)
            u = matmul_cols(h, cd_w_in, o, qw + 2 * kvw, wd_)
            vg = matmul_cols(h, cd_w_in, o, qw + 2 * kvw + wd_, wd_)
            att_p, krot_p = attn_prompt(q, k, v, c_sink[o], n_bp, t)
            att_s, krot_s = attn_sample(q, k, v, mp, n_bs, s, cache_win_k[o], cache_win_v[o], c_sink[o])
            yd_p, _ = gmlp(u, vg, 0, mp, GMLP_CHUNK, d_ln_g[o], d_ln_b[o], d_w_s[o], d_b_s[o], False)
            yd_s, vn_s = gmlp(u, vg, mp, ms, s, d_ln_g[o], d_ln_b[o], d_w_s[o], d_b_s[o], True)
            kv_shape = (N_KV_HEADS, HEAD_DIM)
            win_k_p.append(krot_p.reshape(n_bp, t, *kv_shape)[:, -WINDOW:])
            win_v_p.append(v[:mp].reshape(n_bp, t, *kv_shape)[:, -WINDOW:])
            k_all = jnp.concatenate([cache_win_k[o], krot_s.reshape(n_bs, s, *kv_shape)], axis=1)
            v_all = jnp.concatenate([cache_win_v[o], v[mp:].reshape(n_bs, s, *kv_shape)], axis=1)
            win_k_s.append(k_all[:, -WINDOW:])
            win_v_s.append(v_all[:, -WINDOW:])
            chunk_v_s.append(vn_s.reshape(n_bs, s, wd_))
            y_parts = [(att_p, 0, 0), (yd_p, 0, qw), (att_s, mp, 0), (yd_s, mp, qw)]
            w_out, e_out = cd_out, o
        x, h = matmul_resid_norm(y_parts, w_out, e_out, x, g[3], g[4], 1.0)

        m_n = rms_cast(mem, g[8])
        mk = matmul_cols(m_n, xattn_wk, layer, 0, wx)
        mv = matmul_cols(m_n, xattn_wv, layer, 0, wx)
        mem_k_p.append(mk.reshape(n_bp, n_mem, N_X_HEADS, HEAD_DIM))
        mem_v_p.append(mv.reshape(n_bp, n_mem, N_X_HEADS, HEAD_DIM))
        qx = matmul_cols(h, xattn_wq, layer, 0, wx)
        o_p = cross_attn(qx, 0, n_bp, t, mk.reshape(n_bp, n_mem, wx), mv.reshape(n_bp, n_mem, wx))
        o_s = cross_attn(qx, mp, n_bs, s, cache_mem_k[layer].reshape(n_bs, n_mem, wx),
                         cache_mem_v[layer].reshape(n_bs, n_mem, wx))
        x, h = matmul_resid_norm([(o_p, 0, 0), (o_s, mp, 0)], wo_all, layer, x, g[5], g[6], 1.0)

        if layer + 1 < depth:
            x, h = half_ffn(h, x, layer, 1, g[7], norms[layer + 1, 0])
        else:
            (xo_p, xo_s), _ = half_ffn(h, x, layer, 1, g[7], None, out_rows=[mp, ms])

    return (xo_p.reshape(n_bp, t, d), xo_s.reshape(n_bs, s, d),
            jnp.stack(conv_a_p), jnp.stack(conv_a_s), jnp.stack(conv_b_p), jnp.stack(conv_b_s),
            jnp.stack(win_k_p), jnp.stack(win_v_p), jnp.stack(win_k_s), jnp.stack(win_v_s),
            jnp.stack(chunk_v_s), jnp.stack(mem_k_p), jnp.stack(mem_v_p))
```

```python
import functools
import itertools
import math

import jax
import jax.numpy as jnp
from jax import lax
from jax.experimental import pallas as pl
from jax.experimental.pallas import tpu as pltpu

EPS = 1e-6
CHUNK = 64
HEAD_DIM = 128
WINDOW = 128
ROPE_THETA = 10000.0
GMLP_CHUNK = 128
GMLP_GROUPS = 8
N_KV_HEADS = 4
GQA_GROUP = 4
N_X_HEADS = 4
CONV_A = 3
CONV_B = 31
PAST_LEN = 2048
N_NORMS = 9

V7X_VMEM_BYTES = 64 * 1024 * 1024
VMEM_LIMIT = V7X_VMEM_BYTES - 8 * 1024 * 1024
SUBLANE = 8
LANE = 128
HALO_A = SUBLANE
HALO_B = 4 * SUBLANE

BF16 = jnp.bfloat16
F32 = jnp.float32


def _tile(n, target, mult):
    t = min(n, target)
    t -= t % mult
    while t >= mult:
        if n % t == 0:
            return t
        t -= mult
    return n


def _params(*sem):
    return pltpu.CompilerParams(dimension_semantics=sem, vmem_limit_bytes=VMEM_LIMIT)


def _rms(x, g):
    return x * lax.rsqrt(jnp.mean(x * x, axis=-1, keepdims=True) + EPS) * g


def _layer_norm(x, g, b):
    mu = jnp.mean(x, axis=-1, keepdims=True)
    xc = x - mu
    var = jnp.mean(xc * xc, axis=-1, keepdims=True)
    return xc * lax.rsqrt(var + EPS) * g + b


def _silu(x):
    return x * jax.nn.sigmoid(x)


def _rms_cast2_kernel(xp_ref, xs_ref, g_ref, x_ref, h_ref, *, n_p):
    i = pl.program_id(0)

    @pl.when(i < n_p)
    def _():
        x = xp_ref[...]
        x_ref[...] = x
        h_ref[...] = _rms(x, g_ref[...]).astype(BF16)

    @pl.when(i >= n_p)
    def _():
        x = xs_ref[...]
        x_ref[...] = x
        h_ref[...] = _rms(x, g_ref[...]).astype(BF16)


def rms_cast_join(xp, xs, g):
    (mp, d), ms = xp.shape, xs.shape[0]
    tm = _tile(ms, 256, SUBLANE)
    assert mp % tm == 0
    n_p, n_s = mp // tm, ms // tm
    kern = functools.partial(_rms_cast2_kernel, n_p=n_p)
    row = lambda i: (i, 0)
    return pl.pallas_call(
        kern,
        out_shape=[jax.ShapeDtypeStruct((mp + ms, d), F32), jax.ShapeDtypeStruct((mp + ms, d), BF16)],
        grid=(n_p + n_s,),
        in_specs=[pl.BlockSpec((tm, d), lambda i: (jnp.minimum(i, n_p - 1), 0)),
                  pl.BlockSpec((tm, d), lambda i: (jnp.maximum(i - n_p, 0), 0)),
                  pl.BlockSpec((1, d), lambda i: (0, 0))],
        out_specs=[pl.BlockSpec((tm, d), row), pl.BlockSpec((tm, d), row)],
        compiler_params=_params("parallel"),
        name="rms_cast_join",
    )(xp, xs, g.reshape(1, d))


def _rms_cast_kernel(x_ref, g_ref, o_ref):
    o_ref[...] = _rms(x_ref[...], g_ref[...]).astype(o_ref.dtype)


def rms_cast(x, g):
    m, d = x.shape
    tm = _tile(m, 256, SUBLANE)
    return pl.pallas_call(
        _rms_cast_kernel,
        out_shape=jax.ShapeDtypeStruct((m, d), BF16),
        grid=(m // tm,),
        in_specs=[pl.BlockSpec((tm, d), lambda i: (i, 0)),
                  pl.BlockSpec((1, d), lambda i: (0, 0))],
        out_specs=pl.BlockSpec((tm, d), lambda i: (i, 0)),
        compiler_params=_params("parallel"),
        name="rms_cast",
    )(x, g.reshape(1, d))


ACT_ROWS = 2304


def _act_spec(tm, k):
    return pl.BlockSpec((tm, k), lambda i, j: (i, 0), pipeline_mode=pl.Buffered(1))


def _mm_kernel(a_ref, w_ref, o_ref):
    w = w_ref[...].astype(BF16)
    o_ref[...] = jnp.dot(a_ref[...], w, preferred_element_type=F32).astype(o_ref.dtype)


def matmul_cols(a, w, e, col0, ncols, out_dtype=F32):
    m, k = a.shape
    tn = _tile(ncols, 512, LANE)
    assert col0 % tn == 0
    cb = col0 // tn
    wide = ncols // tn >= 8
    tm = _tile(m, ACT_ROWS if wide else 1024, SUBLANE)
    return pl.pallas_call(
        _mm_kernel,
        out_shape=jax.ShapeDtypeStruct((m, ncols), out_dtype),
        grid=(m // tm, ncols // tn),
        in_specs=[_act_spec(tm, k) if wide else pl.BlockSpec((tm, k), lambda i, j: (i, 0)),
                  pl.BlockSpec((None, k, tn), lambda i, j: (e, 0, cb + j))],
        out_specs=pl.BlockSpec((tm, tn), lambda i, j: (i, j)),
        compiler_params=_params("parallel", "parallel"),
        name="matmul_cols",
    )(a, w)


def _swiglu_kernel(h_ref, wg_ref, wu_ref, o_ref):
    h = h_ref[...]
    g = jnp.dot(h, wg_ref[...].astype(BF16), preferred_element_type=F32)
    u = jnp.dot(h, wu_ref[...].astype(BF16), preferred_element_type=F32)
    o_ref[...] = (_silu(g) * u).astype(o_ref.dtype)


def swiglu_up(h, wg, wu, e):
    m, k = h.shape
    f = wg.shape[2]
    tm = _tile(m, ACT_ROWS, SUBLANE)
    tn = _tile(f, 256, LANE)
    wspec = pl.BlockSpec((None, k, tn), lambda i, j: (e, 0, j))
    return pl.pallas_call(
        _swiglu_kernel,
        out_shape=jax.ShapeDtypeStruct((m, f), BF16),
        grid=(m // tm, f // tn),
        in_specs=[_act_spec(tm, k), wspec, wspec],
        out_specs=pl.BlockSpec((tm, tn), lambda i, j: (i, j)),
        compiler_params=_params("parallel", "parallel"),
        name="swiglu_up",
    )(h, wg, wu)


EPILOGUE_ROWS = 2 * SUBLANE
EPILOGUE_UNROLL = 4


def _mm_resid_kernel(*refs, parts, n_main, has_tail, splits, n_chunk, with_next):
    a_refs, w_ref = refs[:len(parts)], refs[len(parts)]
    refs = refs[len(parts) + 1:]
    if has_tail:
        at_ref, wt_ref = refs[:2]
        refs = refs[2:]
    x_ref, gpost_ref, gnext_ref = refs[:3]
    xo_refs = refs[3:3 + len(splits)]
    ho_ref = refs[3 + len(splits)] if with_next else None
    inv_ref = refs[-1]
    i, k = pl.program_id(0), pl.program_id(1)
    tm, n = xo_refs[0].shape
    nk = n_main + int(has_tail)
    n_tiles = max(s[1] for s in splits)

    def in_range(idx, lo, hi, full):
        conds = ([idx >= lo] if lo > 0 else []) + ([idx < hi] if hi < full else [])
        return functools.reduce(lambda p, q: p & q, conds) if conds else None

    def when(*conds):
        conds = [c for c in conds if c is not None]
        return pl.when(functools.reduce(lambda p, q: p & q, conds)) if conds else (lambda f: f())

    def accumulate(a_blk, w_blk, xo_ref, first):
        a = a_blk[...]
        for c in range(0, n, n_chunk):
            part = jnp.dot(a, w_blk[:, c:c + n_chunk], preferred_element_type=F32)
            if first:
                xo_ref[:, c:c + n_chunk] = part
            else:
                xo_ref[:, c:c + n_chunk] += part

    def epilogue(xo_ref):
        lanes = [slice(j, j + LANE) for j in range(0, n, LANE)]

        def rows_of(c):
            if isinstance(c, int):
                return pl.ds(c * EPILOGUE_ROWS, EPILOGUE_ROWS)
            return pl.ds(pl.multiple_of(c * EPILOGUE_ROWS, EPILOGUE_ROWS), EPILOGUE_ROWS)

        def inv_rms(sq_lanes):
            ms = jnp.sum(sq_lanes, axis=-1, keepdims=True) * (1.0 / n)
            return jnp.broadcast_to(lax.rsqrt(ms + EPS), (EPILOGUE_ROWS, LANE))

        def stats_rows(c):
            rows = rows_of(c)
            sq = jnp.zeros((EPILOGUE_ROWS, LANE), F32)
            for ls in lanes:
                f = xo_ref[rows, ls]
                sq = sq + f * f
            return sq

        def resid_rows(c):
            rows = rows_of(c)
            r = inv_ref[rows, :]
            sq = jnp.zeros((EPILOGUE_ROWS, LANE), F32)
            for ls in lanes:
                xn = x_ref[rows, ls] + xo_ref[rows, ls] * r * gpost_ref[:, ls]
                xo_ref[rows, ls] = xn
                sq = sq + xn * xn
            return sq

        def next_step(c, carry):
            rows = rows_of(c)
            r = inv_ref[rows, :]
            for ls in lanes:
                ho_ref[rows, ls] = (xo_ref[rows, ls] * r * gnext_ref[:, ls]).astype(BF16)
            return carry

        steps = tm // EPILOGUE_ROWS
        u = EPILOGUE_UNROLL
        assert steps % u == 0

        def sweep(rows_fn, keep_stats):
            def batch(b):
                return tuple(rows_fn(b * u + j) for j in range(u))

            def finish(b, sqs):
                if keep_stats:
                    for j in range(u):
                        inv_ref[rows_of(b * u + j), :] = inv_rms(sqs[j])

            def step(b, sqs_prev):
                sqs = batch(b)
                finish(b - 1, sqs_prev)
                return sqs

            n_batches = steps // u
            finish(n_batches - 1, lax.fori_loop(1, n_batches, step, batch(0)))

        sweep(stats_rows, True)
        sweep(resid_rows, with_next)
        if with_next:
            lax.fori_loop(0, steps, next_step, 0, unroll=u)

    for (s0, s1), xo_ref in zip(splits, xo_refs):
        for (r0, r1, k0, k1), a_ref in zip(parts, a_refs):
            lo, hi = max(r0, s0), min(r1, s1)
            if lo >= hi:
                continue
            rows_ok = in_range(i, lo, hi, n_tiles)
            if k0 == 0:
                when(rows_ok, k == 0)(functools.partial(accumulate, a_ref, w_ref, xo_ref, True))
                if k1 > 1:
                    when(rows_ok, k > 0, in_range(k, 0, k1, nk))(
                        functools.partial(accumulate, a_ref, w_ref, xo_ref, False))
            else:
                when(rows_ok, in_range(k, k0, k1, nk))(functools.partial(accumulate, a_ref, w_ref, xo_ref, False))
        rows_ok = in_range(i, s0, s1, n_tiles)
        if has_tail:
            when(rows_ok, k == n_main)(functools.partial(accumulate, at_ref, wt_ref, xo_ref, False))
        when(rows_ok, k == nk - 1)(functools.partial(epilogue, xo_ref))


def matmul_resid_norm(a_parts, w, e, x, g_post, g_next, scale, out_rows=None):
    m, n = x.shape
    kdim = w.shape[1]
    edges = [row0 for _, row0, _ in a_parts] + list(itertools.accumulate(out_rows or []))
    tm = _tile(math.gcd(m, *edges), 512, EPILOGUE_ROWS)
    tk = min(512, kdim)
    n_main, tail = kdim // tk, kdim % tk
    has_tail = tail > 0
    nk = n_main + int(has_tail)
    with_next = g_next is not None
    assert not (with_next and out_rows is not None)
    gn = g_next if with_next else g_post
    gp = g_post if scale == 1.0 else g_post * scale
    row = lambda i, k: (i, 0)
    fixed = lambda i, k: (0, 0)

    def clipped(lo, hi):
        return lambda v: jnp.clip(v - lo, 0, hi - lo - 1)

    parts, in_specs, operands = [], [], []
    for arr, row0, col0 in a_parts:
        rows, cols = arr.shape
        assert row0 % tm == 0 and rows % tm == 0 and col0 % tk == 0
        assert cols % tk == 0 or (len(a_parts) == 1 and has_tail)
        part = (row0 // tm, (row0 + rows) // tm, col0 // tk, (col0 + cols) // tk)
        ri, ki = clipped(part[0], part[1]), clipped(part[2], part[3])
        parts.append(part)
        in_specs.append(pl.BlockSpec((tm, tk), lambda i, k, ri=ri, ki=ki: (ri(i), ki(k))))
        operands.append(arr)
    in_specs.append(pl.BlockSpec((None, tk, n), lambda i, k: (e, jnp.minimum(k, n_main - 1), 0)))
    operands.append(w)
    if has_tail:
        assert tail % LANE == 0 and (n_main * tk) % tail == 0
        tail_blk = (n_main * tk) // tail
        in_specs += [pl.BlockSpec((tm, tail), lambda i, k: (i, tail_blk)),
                     pl.BlockSpec((None, tail, n), lambda i, k: (e, tail_blk, 0))]
        operands += [a_parts[0][0], w]
    x_mode = {} if out_rows is None else dict(pipeline_mode=pl.Buffered(1))
    in_specs += [pl.BlockSpec((tm, n), row, **x_mode), pl.BlockSpec((1, n), fixed), pl.BlockSpec((1, n), fixed)]
    operands += [x, gp.reshape(1, n), gn.reshape(1, n)]

    splits, out_shape, out_specs, row0 = [], [], [], 0
    for rws in ([m] if out_rows is None else out_rows):
        split = (row0 // tm, (row0 + rws) // tm)
        ri = clipped(*split)
        splits.append(split)
        out_shape.append(jax.ShapeDtypeStruct((rws, n), F32))
        out_specs.append(pl.BlockSpec((tm, n), lambda i, k, ri=ri: (ri(i), 0)))
        row0 += rws
    if with_next:
        out_shape.append(jax.ShapeDtypeStruct((m, n), BF16))
        out_specs.append(pl.BlockSpec((tm, n), row))
    kern = functools.partial(_mm_resid_kernel, parts=tuple(parts), n_main=n_main, has_tail=has_tail,
                             splits=tuple(splits), n_chunk=_tile(n, 1024, LANE), with_next=with_next)
    outs = pl.pallas_call(
        kern,
        out_shape=out_shape,
        grid=(m // tm, nk),
        in_specs=in_specs,
        out_specs=out_specs,
        scratch_shapes=[pltpu.VMEM((tm, LANE), F32)],
        compiler_params=_params("arbitrary", "arbitrary"),
        name="matmul_resid_norm",
    )(*operands)
    if out_rows is not None:
        return tuple(outs), None
    return (outs[0], outs[1]) if with_next else (outs[0], None)


def _tap_rows(w):
    return jnp.repeat(w, SUBLANE, axis=0)


def _tap_weight(w_ref, j, cols, rows):
    tile = w_ref[j * SUBLANE:(j + 1) * SUBLANE, cols]
    return jnp.concatenate([tile] * (rows // SUBLANE), axis=0)


def _ab_block(go, gi, val, ga, gg, ca_halo, cb_halo, aw_ref, bw_ref, bias, lng, lnb,
              exta_ref, extb_ref, z_ref):
    r, c = go.shape
    ca = gi * val
    exta_ref[0:HALO_A, :] = ca_halo
    exta_ref[HALO_A:HALO_A + r, :] = ca
    conv_a = _tap_weight(aw_ref, CONV_A - 1, slice(None), r) * ca
    for j in range(CONV_A - 1):
        off = HALO_A - (CONV_A - 1) + j
        conv_a = conv_a + _tap_weight(aw_ref, j, slice(None), r) * exta_ref[off:off + r, :]
    y_a = go * conv_a
    cb = ga * jax.nn.sigmoid(gg)
    extb_ref[0:HALO_B, :] = cb_halo
    extb_ref[HALO_B:HALO_B + r, :] = cb
    rc = min(r, 64)
    cc = min(c, 256)
    base = HALO_B - (CONV_B - 1)
    phases = {}
    for j in range(CONV_B):
        phases.setdefault((base + j) % SUBLANE, []).append(j)
    for r0 in range(0, r, rc):
        for c0 in range(0, c, cc):
            cols = slice(c0, c0 + cc)
            acc = jnp.broadcast_to(bias[:, cols], (rc, cc))
            for b, taps in sorted(phases.items()):
                span = SUBLANE * max((base + j) // SUBLANE for j in taps) + rc
                win = extb_ref[r0 + b:r0 + b + span, cols]
                for j in taps:
                    a = SUBLANE * ((base + j) // SUBLANE)
                    acc = acc + _tap_weight(bw_ref, j, cols, rc) * win[a:a + rc]
            z_ref[r0:r0 + rc, cols] = acc
    y_b = _silu(_layer_norm(z_ref[0:r, :], lng, lnb))
    return y_a, y_b, ca, cb


def _ab_prompt_kernel(go_ref, gi_ref, val_ref, ga_ref, gg_ref, gih_ref, valh_ref, gah_ref, ggh_ref,
                      aw_ref, bw_ref, bias_ref, lng_ref, lnb_ref,
                      y_ref, ca_tail_ref, cb_tail_ref, exta_ref, extb_ref, z_ref, *, n_i):
    i = pl.program_id(1)
    r, c = go_ref.shape
    first = i == 0
    ca_halo = jnp.where(first, 0.0, gih_ref[...] * valh_ref[...])
    cb_halo = jnp.where(first, 0.0, gah_ref[...] * jax.nn.sigmoid(ggh_ref[...]))
    y_a, y_b, ca, cb = _ab_block(go_ref[...], gi_ref[...], val_ref[...], ga_ref[...], gg_ref[...],
                                 ca_halo, cb_halo, aw_ref, bw_ref, bias_ref[...], lng_ref[...], lnb_ref[...],
                                 exta_ref, extb_ref, z_ref)
    y_ref[:, 0:c] = y_a.astype(BF16)
    y_ref[:, c:2 * c] = y_b.astype(BF16)

    @pl.when(i == n_i - 1)
    def _():
        ca_tail_ref[0] = exta_ref[r:r + HALO_A, :]
        cb_tail_ref[0] = extb_ref[r:r + HALO_B, :]


AB_SECTIONS = 5


def mixer_ab_prompt(proj, n_b, t, a_w, b_w, b_bias, ln_g, ln_b):
    c = proj.shape[1] // AB_SECTIONS
    tr = _tile(t, 256, HALO_B)
    n_i = t // tr

    def cur(sec):
        return pl.BlockSpec((tr, c), lambda b, i: (b * n_i + i, sec))

    def halo(rows, sec):
        per = tr // rows
        return pl.BlockSpec((rows, c), lambda b, i: (jnp.maximum((b * n_i + i) * per - 1, 0), sec))

    def vec(rows):
        return pl.BlockSpec((rows, c), lambda b, i: (0, 0))

    kern = functools.partial(_ab_prompt_kernel, n_i=n_i)
    return pl.pallas_call(
        kern,
        out_shape=[jax.ShapeDtypeStruct((n_b * t, 2 * c), BF16),
                   jax.ShapeDtypeStruct((n_b, HALO_A, c), F32),
                   jax.ShapeDtypeStruct((n_b, HALO_B, c), F32)],
        grid=(n_b, n_i),
        in_specs=[cur(0), cur(1), cur(2), cur(3), cur(4),
                  halo(HALO_A, 1), halo(HALO_A, 2), halo(HALO_B, 3), halo(HALO_B, 4),
                  vec(CONV_A * SUBLANE), vec(CONV_B * SUBLANE), vec(1), vec(1), vec(1)],
        out_specs=[pl.BlockSpec((tr, 2 * c), lambda b, i: (b * n_i + i, 0)),
                   pl.BlockSpec((1, HALO_A, c), lambda b, i: (b, 0, 0)),
                   pl.BlockSpec((1, HALO_B, c), lambda b, i: (b, 0, 0))],
        scratch_shapes=[pltpu.VMEM((HALO_A + tr, c), F32), pltpu.VMEM((HALO_B + tr, c), F32),
                        pltpu.VMEM((tr, c), F32)],
        compiler_params=_params("parallel", "arbitrary"),
        name="mixer_ab_prompt",
    )(*([proj] * 9), _tap_rows(a_w), _tap_rows(b_w),
      b_bias.reshape(1, c), ln_g.reshape(1, c), ln_b.reshape(1, c))


def _ab_sample_kernel(go_ref, gi_ref, val_ref, ga_ref, gg_ref, ha_ref, hb_ref,
                      aw_ref, bw_ref, bias_ref, lng_ref, lnb_ref,
                      y_ref, ca_tail_ref, cb_tail_ref, exta_ref, extb_ref, z_ref, *, s):
    nb = ha_ref.shape[0]
    c = go_ref.shape[1]
    def one_stream(q, carry):
        rows = pl.ds(pl.multiple_of(q * s, s), s)
        y_a, y_b, ca, cb = _ab_block(go_ref[rows, :], gi_ref[rows, :], val_ref[rows, :], ga_ref[rows, :],
                                     gg_ref[rows, :], ha_ref[q], hb_ref[q], aw_ref, bw_ref, bias_ref[...],
                                     lng_ref[...], lnb_ref[...], exta_ref, extb_ref, z_ref)
        y_ref[rows, 0:c] = y_a.astype(BF16)
        y_ref[rows, c:2 * c] = y_b.astype(BF16)
        ca_tail_ref[q] = exta_ref[s:s + HALO_A, :]
        cb_tail_ref[q] = extb_ref[s:s + HALO_B, :]
        return carry

    lax.fori_loop(0, nb, one_stream, 0)


def mixer_ab_sample(proj, row0, n_b, s, hist_a, hist_b, a_w, b_w, b_bias, ln_g, ln_b):
    c = proj.shape[1] // AB_SECTIONS
    assert s >= HALO_B
    nb = _tile(n_b, 8, 1)
    assert row0 % (nb * s) == 0
    blk0 = row0 // (nb * s)
    ha = jnp.pad(hist_a, ((0, 0), (HALO_A - (CONV_A - 1), 0), (0, 0)))
    hb = jnp.pad(hist_b, ((0, 0), (HALO_B - (CONV_B - 1), 0), (0, 0)))
    def cur(sec):
        return pl.BlockSpec((nb * s, c), lambda i: (blk0 + i, sec))

    def vec(rows):
        return pl.BlockSpec((rows, c), lambda i: (0, 0))

    kern = functools.partial(_ab_sample_kernel, s=s)
    return pl.pallas_call(
        kern,
        out_shape=[jax.ShapeDtypeStruct((n_b * s, 2 * c), BF16),
                   jax.ShapeDtypeStruct((n_b, HALO_A, c), F32),
                   jax.ShapeDtypeStruct((n_b, HALO_B, c), F32)],
        grid=(n_b // nb,),
        in_specs=[cur(0), cur(1), cur(2), cur(3), cur(4),
                  pl.BlockSpec((nb, HALO_A, c), lambda i: (i, 0, 0)),
                  pl.BlockSpec((nb, HALO_B, c), lambda i: (i, 0, 0)),
                  vec(CONV_A * SUBLANE), vec(CONV_B * SUBLANE), vec(1), vec(1), vec(1)],
        out_specs=[pl.BlockSpec((nb * s, 2 * c), lambda i: (i, 0)),
                   pl.BlockSpec((nb, HALO_A, c), lambda i: (i, 0, 0)),
                   pl.BlockSpec((nb, HALO_B, c), lambda i: (i, 0, 0))],
        scratch_shapes=[pltpu.VMEM((HALO_A + s, c), F32), pltpu.VMEM((HALO_B + s, c), F32),
                        pltpu.VMEM((s, c), F32)],
        compiler_params=_params("parallel"),
        name="mixer_ab_sample",
    )(*([proj] * 5), ha, hb, _tap_rows(a_w), _tap_rows(b_w),
      b_bias.reshape(1, c), ln_g.reshape(1, c), ln_b.reshape(1, c))


def _rope_tables(pos):
    half = HEAD_DIM // 2
    inv_freq = ROPE_THETA ** (-jnp.arange(half, dtype=F32) / half)
    ang = pos.astype(F32)[:, None] * inv_freq[None, :]
    cos = jnp.cos(ang)
    sin = jnp.sin(ang)
    return jnp.concatenate([cos, cos], axis=-1), jnp.concatenate([-sin, sin], axis=-1)


def _rope(x, cos2, sin2):
    return x * cos2 + pltpu.roll(x, HEAD_DIM // 2, 1) * sin2


def _sink_softmax_pv(s, sink, vv):
    m = jnp.maximum(jnp.max(s, axis=-1, keepdims=True), sink)
    e = jnp.exp(s - m)
    p = e / (jnp.sum(e, axis=-1, keepdims=True) + jnp.exp(sink - m))
    return jnp.dot(p.astype(BF16), vv, preferred_element_type=F32)


def _qk(q, kk):
    return lax.dot_general(q, kk, (((1,), (1,)), ((), ())), preferred_element_type=F32) * (HEAD_DIM ** -0.5)


def _attn_prompt_kernel(sink_ref, q_ref, k_ref, v_ref, kh_ref, vh_ref, cos_ref, sin_ref, cosh_ref, sinh_ref,
                        att_ref, krot_ref):
    i = pl.program_id(1)
    tq = q_ref.shape[0]
    cos, sin = cos_ref[...], sin_ref[...]
    cosh, sinh = cosh_ref[...], sinh_ref[...]
    n_keys = WINDOW + tq
    q_chunk = lax.broadcasted_iota(jnp.int32, (tq, n_keys), 0) // CHUNK
    k_col = lax.broadcasted_iota(jnp.int32, (tq, n_keys), 1)
    k_chunk = k_col // CHUNK
    n_back = WINDOW // CHUNK
    mask = (k_chunk >= q_chunk) & (k_chunk <= q_chunk + n_back)
    mask = mask & (k_col >= jnp.where(i > 0, 0, WINDOW))
    for h in range(N_KV_HEADS):
        hs = slice(h * HEAD_DIM, (h + 1) * HEAD_DIM)
        k_cur = _rope(k_ref[:, hs], cos, sin)
        krot_ref[:, hs] = k_cur
        kk = jnp.concatenate([_rope(kh_ref[:, hs], cosh, sinh), k_cur], axis=0).astype(BF16)
        vv = jnp.concatenate([vh_ref[:, hs], v_ref[:, hs]], axis=0).astype(BF16)
        for g in range(GQA_GROUP):
            n = h * GQA_GROUP + g
            ns = slice(n * HEAD_DIM, (n + 1) * HEAD_DIM)
            qn = _rope(q_ref[:, ns], cos, sin).astype(BF16)
            s = jnp.where(mask, _qk(qn, kk), -jnp.inf)
            att_ref[:, ns] = _sink_softmax_pv(s, sink_ref[n], vv).astype(BF16)


def attn_prompt(proj, qw, kw, sink, n_b, t):
    assert qw % kw == 0
    kb = qw // kw
    tq = _tile(t, 256, WINDOW)
    n_i = t // tq
    per = tq // WINDOW
    cos2, sin2 = _rope_tables(jnp.arange(t, dtype=jnp.int32))
    cur = lambda w, cb=0: pl.BlockSpec((tq, w), lambda b, i: (b * n_i + i, cb))
    halo = lambda cb: pl.BlockSpec((WINDOW, kw), lambda b, i: (jnp.maximum((b * n_i + i) * per - 1, 0), cb))
    tab = pl.BlockSpec((tq, HEAD_DIM), lambda b, i: (i, 0))
    tab_h = pl.BlockSpec((WINDOW, HEAD_DIM), lambda b, i: (jnp.maximum(i * per - 1, 0), 0))
    return pl.pallas_call(
        _attn_prompt_kernel,
        out_shape=[jax.ShapeDtypeStruct((n_b * t, qw), BF16), jax.ShapeDtypeStruct((n_b * t, kw), F32)],
        grid=(n_b, n_i),
        in_specs=[pl.BlockSpec(memory_space=pltpu.SMEM), cur(qw), cur(kw, kb), cur(kw, kb + 1),
                  halo(kb), halo(kb + 1), tab, tab, tab_h, tab_h],
        out_specs=[cur(qw), cur(kw)],
        compiler_params=_params("parallel", "parallel"),
        name="attn_prompt",
    )(sink.reshape(-1), proj, proj, proj, proj, proj, cos2, sin2, cos2, sin2)


def _attn_sample_kernel(sink_ref, q_ref, k_ref, v_ref, wk_ref, wv_ref, cos_ref, sin_ref, att_ref, krot_ref, *, s):
    nb = wk_ref.shape[0]
    cos, sin = cos_ref[...], sin_ref[...]
    def one_stream(b, carry):
        rows = pl.ds(pl.multiple_of(b * s, s), s)
        for h in range(N_KV_HEADS):
            hs = slice(h * HEAD_DIM, (h + 1) * HEAD_DIM)
            k_cur = _rope(k_ref[rows, hs], cos, sin)
            krot_ref[rows, hs] = k_cur
            kk = jnp.concatenate([wk_ref[b, :, hs], k_cur], axis=0).astype(BF16)
            vv = jnp.concatenate([wv_ref[b, :, hs], v_ref[rows, hs]], axis=0).astype(BF16)
            for g in range(GQA_GROUP):
                n = h * GQA_GROUP + g
                ns = slice(n * HEAD_DIM, (n + 1) * HEAD_DIM)
                qn = _rope(q_ref[rows, ns], cos, sin).astype(BF16)
                att_ref[rows, ns] = _sink_softmax_pv(_qk(qn, kk), sink_ref[n], vv).astype(BF16)
        return carry

    lax.fori_loop(0, nb, one_stream, 0)


def attn_sample(proj, qw, kw, row0, n_b, s, win_k, win_v, sink):
    assert qw % kw == 0
    kb = qw // kw
    nb = _tile(n_b, 8, 1)
    assert row0 % (nb * s) == 0
    blk0 = row0 // (nb * s)
    cos2, sin2 = _rope_tables(PAST_LEN + jnp.arange(s, dtype=jnp.int32))
    cur = lambda w, cb=0: pl.BlockSpec((nb * s, w), lambda i: (blk0 + i, cb))
    out = lambda w: pl.BlockSpec((nb * s, w), lambda i: (i, 0))
    win = pl.BlockSpec((nb, WINDOW, kw), lambda i: (i, 0, 0))
    tab = pl.BlockSpec((s, HEAD_DIM), lambda i: (0, 0))
    kern = functools.partial(_attn_sample_kernel, s=s)
    return pl.pallas_call(
        kern,
        out_shape=[jax.ShapeDtypeStruct((n_b * s, qw), BF16), jax.ShapeDtypeStruct((n_b * s, kw), F32)],
        grid=(n_b // nb,),
        in_specs=[pl.BlockSpec(memory_space=pltpu.SMEM), cur(qw), cur(kw, kb), cur(kw, kb + 1), win, win, tab, tab],
        out_specs=[out(qw), out(kw)],
        compiler_params=_params("parallel"),
        name="attn_sample",
    )(sink.reshape(-1), proj, proj, proj, win_k.reshape(n_b, WINDOW, kw), win_v.reshape(n_b, WINDOW, kw), cos2, sin2)


def _gmlp_kernel(u0_ref, u1_ref, vg0_ref, vg1_ref, lng_ref, lnb_ref, ws_ref, bst_ref, y_ref, *rest, with_vn):
    tr, half = u0_ref.shape
    c = 2 * half
    r = ws_ref.shape[1]
    gd = c // GMLP_GROUPS
    vn = _layer_norm(jnp.concatenate([vg0_ref[...], vg1_ref[...]], axis=1), lng_ref[...], lnb_ref[...])
    if with_vn:
        rest[0][...] = vn
    vnb = vn.astype(BF16)
    row_chunk = lax.broadcasted_iota(jnp.int32, (r, r), 0) // CHUNK
    col_chunk = lax.broadcasted_iota(jnp.int32, (r, r), 1) // CHUNK
    allowed = col_chunk <= row_chunk
    for g in range(GMLP_GROUPS):
        wg = jnp.where(allowed, ws_ref[g], 0.0).astype(BF16)
        bias = bst_ref[:, g:g + 1]
        cs = slice(g * gd, (g + 1) * gd)
        u_ref = u0_ref if g * gd < half else u1_ref
        us = slice((g * gd) % half, (g * gd) % half + gd)
        for r0 in range(0, tr, r):
            sg = jnp.dot(wg, vnb[r0:r0 + r, cs], preferred_element_type=F32) + bias
            y_ref[r0:r0 + r, cs] = (u_ref[r0:r0 + r, us] * sg).astype(BF16)


def gmlp(proj, col_u, col_vg, c, row0, n_rows, r, ln_g, ln_b, w_s, b_s, with_vn):
    half = c // 2
    assert col_u % half == 0 and col_vg % half == 0 and half % (c // GMLP_GROUPS) == 0
    tr = _tile(n_rows, 256, r)
    assert row0 % tr == 0
    blk0 = row0 // tr
    ws = w_s[:, :r, :r]
    bst = b_s[:, :r].T
    cur = lambda cb: pl.BlockSpec((tr, half), lambda i: (blk0 + i, cb))
    out = pl.BlockSpec((tr, c), lambda i: (i, 0))
    vec = pl.BlockSpec((1, c), lambda i: (0, 0))
    out_shape = [jax.ShapeDtypeStruct((n_rows, c), BF16)]
    out_specs = [out]
    if with_vn:
        out_shape.append(jax.ShapeDtypeStruct((n_rows, c), F32))
        out_specs.append(out)
    kern = functools.partial(_gmlp_kernel, with_vn=with_vn)
    outs = pl.pallas_call(
        kern,
        out_shape=out_shape,
        grid=(n_rows // tr,),
        in_specs=[cur(col_u // half), cur(col_u // half + 1), cur(col_vg // half), cur(col_vg // half + 1), vec, vec,
                  pl.BlockSpec((GMLP_GROUPS, r, r), lambda i: (0, 0, 0)),
                  pl.BlockSpec((r, GMLP_GROUPS), lambda i: (0, 0))],
        out_specs=out_specs,
        compiler_params=_params("parallel"),
        name="gmlp",
    )(proj, proj, proj, proj, ln_g.reshape(1, c), ln_b.reshape(1, c), ws, bst)
    return (outs[0], outs[1]) if with_vn else (outs[0], None)


def _xattn_kernel(q_ref, mk_ref, mv_ref, o_ref, *, tq):
    nb = mk_ref.shape[0]
    for b in range(nb):
        rows = slice(b * tq, (b + 1) * tq)
        for h in range(N_X_HEADS):
            hs = slice(h * HEAD_DIM, (h + 1) * HEAD_DIM)
            s = _qk(q_ref[rows, hs].astype(BF16), mk_ref[b, :, hs].astype(BF16))
            m = jnp.max(s, axis=-1, keepdims=True)
            e = jnp.exp(s - m)
            p = e / jnp.sum(e, axis=-1, keepdims=True)
            o = jnp.dot(p.astype(BF16), mv_ref[b, :, hs].astype(BF16), preferred_element_type=F32)
            o_ref[rows, hs] = o.astype(BF16)


def cross_attn(q, row0, n_b, t, mk, mv):
    w = q.shape[1]
    n_mem = mk.shape[1]
    if t >= 512:
        nb, tq = 1, _tile(t, 512, SUBLANE)
    else:
        nb, tq = _tile(n_b, 8, 1), t
    n_i = t // tq
    assert row0 % (nb * tq) == 0
    blk0 = row0 // (nb * tq)
    kern = functools.partial(_xattn_kernel, tq=tq)
    mem = pl.BlockSpec((nb, n_mem, w), lambda b, i: (b, 0, 0))
    return pl.pallas_call(
        kern,
        out_shape=jax.ShapeDtypeStruct((n_b * t, w), BF16),
        grid=(n_b // nb, n_i),
        in_specs=[pl.BlockSpec((nb * tq, w), lambda b, i: (blk0 + b * n_i + i, 0)), mem, mem],
        out_specs=pl.BlockSpec((nb * tq, w), lambda b, i: (b * n_i + i, 0)),
        compiler_params=_params("parallel", "parallel"),
        name="cross_attn",
    )(q, mk, mv)


def _stack3(w):
    return w.reshape((-1,) + w.shape[-2:])


def kernel(x_prompt, x_sample, mem_prompt, state_conv_a, state_conv_b, cache_win_k, cache_win_v, cache_mem_k, cache_mem_v, norms, ffn_w_gate, ffn_w_up, ffn_w_down, xattn_wq, xattn_wk, xattn_wv, xattn_wo, ab_w_in, a_conv_w, b_conv_w, b_conv_bias, b_ln_g, b_ln_b, ab_w_out, cd_w_in, c_sink, d_ln_g, d_ln_b, d_w_s, d_b_s, cd_w_out):
    n_bp, t, d = x_prompt.shape
    n_bs, s, _ = x_sample.shape
    depth = norms.shape[0]
    mp, ms = n_bp * t, n_bs * s
    n_mem = mem_prompt.shape[1]
    wx = xattn_wq.shape[2]

    x, h = rms_cast_join(x_prompt.reshape(mp, d), x_sample.reshape(ms, d), norms[0, 0])
    mem = mem_prompt.reshape(n_bp * n_mem, d)

    wg_all, wu_all = _stack3(ffn_w_gate), _stack3(ffn_w_up)
    wd_all = _stack3(ffn_w_down).astype(BF16)
    ab_out, cd_out, wo_all = ab_w_out.astype(BF16), cd_w_out.astype(BF16), xattn_wo.astype(BF16)

    def half_ffn(h, x, layer, half, g_post, g_next, out_rows=None):
        e = 2 * layer + half
        a = swiglu_up(h, wg_all, wu_all, e)
        return matmul_resid_norm([(a, 0, 0)], wd_all, e, x, g_post, g_next, 0.5, out_rows)

    conv_a_p, conv_a_s, conv_b_p, conv_b_s = [], [], [], []
    win_k_p, win_v_p, win_k_s, win_v_s, chunk_v_s = [], [], [], [], []
    mem_k_p, mem_v_p = [], []
    for layer in range(depth):
        g = norms[layer]
        x, h = half_ffn(h, x, layer, 0, g[1], g[2])

        if layer % 2 == 0:
            e = layer // 2
            proj = matmul_cols(h, ab_w_in, e, 0, ab_w_in.shape[2])
            conv = (a_conv_w[e], b_conv_w[e], b_conv_bias[e], b_ln_g[e], b_ln_b[e])
            y_p, ca_p, cb_p = mixer_ab_prompt(proj, n_bp, t, *conv)
            y_s, ca_s, cb_s = mixer_ab_sample(proj, mp, n_bs, s, state_conv_a[e], state_conv_b[e], *conv)
            conv_a_p.append(ca_p[:, HALO_A - (CONV_A - 1):])
            conv_a_s.append(ca_s[:, HALO_A - (CONV_A - 1):])
            conv_b_p.append(cb_p[:, HALO_B - (CONV_B - 1):])
            conv_b_s.append(cb_s[:, HALO_B - (CONV_B - 1):])
            y_parts = [(y_p, 0, 0), (y_s, mp, 0)]
            w_out, e_out = ab_out, e
        else:
            o = layer // 2
            wd_ = d_ln_g.shape[1]
            kvw = N_KV_HEADS * HEAD_DIM
            qw = cd_w_in.shape[2] - 2 * kvw - 2 * wd_
            proj = matmul_cols(h, cd_w_in, o, 0, cd_w_in.shape[2])
            col_v, col_u = qw + kvw, qw + 2 * kvw
            att_p, krot_p = attn_prompt(proj, qw, kvw, c_sink[o], n_bp, t)
            att_s, krot_s = attn_sample(proj, qw, kvw, mp, n_bs, s, cache_win_k[o], cache_win_v[o], c_sink[o])
            gm = (d_ln_g[o], d_ln_b[o], d_w_s[o], d_b_s[o])
            yd_p, _ = gmlp(proj, col_u, col_u + wd_, wd_, 0, mp, GMLP_CHUNK, *gm, False)
            yd_s, vn_s = gmlp(proj, col_u, col_u + wd_, wd_, mp, ms, s, *gm, True)
            kv_shape = (N_KV_HEADS, HEAD_DIM)
            win_k_p.append(krot_p.reshape(n_bp, t, *kv_shape)[:, -WINDOW:])
            v_tail = jnp.stack([proj[(b + 1) * t - WINDOW:(b + 1) * t, col_v:col_u] for b in range(n_bp)])
            win_v_p.append(v_tail.reshape(n_bp, WINDOW, *kv_shape))
            k_all = jnp.concatenate([cache_win_k[o], krot_s.reshape(n_bs, s, *kv_shape)], axis=1)
            v_all = jnp.concatenate([cache_win_v[o], proj[mp:, col_v:col_u].reshape(n_bs, s, *kv_shape)], axis=1)
            win_k_s.append(k_all[:, -WINDOW:])
            win_v_s.append(v_all[:, -WINDOW:])
            chunk_v_s.append(vn_s.reshape(n_bs, s, wd_))
            y_parts = [(att_p, 0, 0), (yd_p, 0, qw), (att_s, mp, 0), (yd_s, mp, qw)]
            w_out, e_out = cd_out, o
        x, h = matmul_resid_norm(y_parts, w_out, e_out, x, g[3], g[4], 1.0)

        m_n = rms_cast(mem, g[8])
        mk = matmul_cols(m_n, xattn_wk, layer, 0, wx)
        mv = matmul_cols(m_n, xattn_wv, layer, 0, wx)
        mem_k_p.append(mk.reshape(n_bp, n_mem, N_X_HEADS, HEAD_DIM))
        mem_v_p.append(mv.reshape(n_bp, n_mem, N_X_HEADS, HEAD_DIM))
        qx = matmul_cols(h, xattn_wq, layer, 0, wx)
        o_p = cross_attn(qx, 0, n_bp, t, mk.reshape(n_bp, n_mem, wx), mv.reshape(n_bp, n_mem, wx))
        o_s = cross_attn(qx, mp, n_bs, s, cache_mem_k[layer].reshape(n_bs, n_mem, wx),
                         cache_mem_v[layer].reshape(n_bs, n_mem, wx))
        x, h = matmul_resid_norm([(o_p, 0, 0), (o_s, mp, 0)], wo_all, layer, x, g[5], g[6], 1.0)

        if layer + 1 < depth:
            x, h = half_ffn(h, x, layer, 1, g[7], norms[layer + 1, 0])
        else:
            (xo_p, xo_s), _ = half_ffn(h, x, layer, 1, g[7], None, out_rows=[mp, ms])

    return (xo_p.reshape(n_bp, t, d), xo_s.reshape(n_bs, s, d),
            jnp.stack(conv_a_p), jnp.stack(conv_a_s), jnp.stack(conv_b_p), jnp.stack(conv_b_s),
            jnp.stack(win_k_p), jnp.stack(win_v_p), jnp.stack(win_k_s), jnp.stack(win_v_s),
            jnp.stack(chunk_v_s), jnp.stack(mem_k_p), jnp.stack(mem_v_p))
```

```python
import functools
import itertools
import math

import jax
import jax.numpy as jnp
from jax import lax
from jax.experimental import pallas as pl
from jax.experimental.pallas import tpu as pltpu

EPS = 1e-6
CHUNK = 64
HEAD_DIM = 128
WINDOW = 128
ROPE_THETA = 10000.0
GMLP_CHUNK = 128
GMLP_GROUPS = 8
N_KV_HEADS = 4
GQA_GROUP = 4
N_X_HEADS = 4
CONV_A = 3
CONV_B = 31
PAST_LEN = 2048
N_NORMS = 9

V7X_VMEM_BYTES = 64 * 1024 * 1024
VMEM_LIMIT = V7X_VMEM_BYTES - 8 * 1024 * 1024
VMEM_LIMIT_SPLIT_OUT = V7X_VMEM_BYTES - 3 * 1024 * 1024
SUBLANE = 8
LANE = 128
HALO_A = SUBLANE
HALO_B = 4 * SUBLANE

BF16 = jnp.bfloat16
F32 = jnp.float32


def _tile(n, target, mult):
    t = min(n, target)
    t -= t % mult
    while t >= mult:
        if n % t == 0:
            return t
        t -= mult
    return n


def _params(*sem, vmem=VMEM_LIMIT):
    return pltpu.CompilerParams(dimension_semantics=sem, vmem_limit_bytes=vmem)


def _rms(x, g):
    return x * lax.rsqrt(jnp.mean(x * x, axis=-1, keepdims=True) + EPS) * g


def _layer_norm(x, g, b):
    mu = jnp.mean(x, axis=-1, keepdims=True)
    xc = x - mu
    var = jnp.mean(xc * xc, axis=-1, keepdims=True)
    return xc * lax.rsqrt(var + EPS) * g + b


def _silu(x):
    return x * jax.nn.sigmoid(x)


def _rms_cast2_kernel(xp_ref, xs_ref, g_ref, x_ref, h_ref, *, n_p):
    i = pl.program_id(0)

    @pl.when(i < n_p)
    def _():
        x = xp_ref[...]
        x_ref[...] = x
        h_ref[...] = _rms(x, g_ref[...]).astype(BF16)

    @pl.when(i >= n_p)
    def _():
        x = xs_ref[...]
        x_ref[...] = x
        h_ref[...] = _rms(x, g_ref[...]).astype(BF16)


def rms_cast_join(xp, xs, g):
    (mp, d), ms = xp.shape, xs.shape[0]
    tm = _tile(ms, 256, SUBLANE)
    assert mp % tm == 0
    n_p, n_s = mp // tm, ms // tm
    kern = functools.partial(_rms_cast2_kernel, n_p=n_p)
    row = lambda i: (i, 0)
    return pl.pallas_call(
        kern,
        out_shape=[jax.ShapeDtypeStruct((mp + ms, d), F32), jax.ShapeDtypeStruct((mp + ms, d), BF16)],
        grid=(n_p + n_s,),
        in_specs=[pl.BlockSpec((tm, d), lambda i: (jnp.minimum(i, n_p - 1), 0)),
                  pl.BlockSpec((tm, d), lambda i: (jnp.maximum(i - n_p, 0), 0)),
                  pl.BlockSpec((1, d), lambda i: (0, 0))],
        out_specs=[pl.BlockSpec((tm, d), row), pl.BlockSpec((tm, d), row)],
        compiler_params=_params("parallel"),
        name="rms_cast_join",
    )(xp, xs, g.reshape(1, d))


def _rms_cast_kernel(x_ref, g_ref, o_ref):
    o_ref[...] = _rms(x_ref[...], g_ref[...]).astype(o_ref.dtype)


def rms_cast(x, g):
    m, d = x.shape
    tm = _tile(m, 256, SUBLANE)
    return pl.pallas_call(
        _rms_cast_kernel,
        out_shape=jax.ShapeDtypeStruct((m, d), BF16),
        grid=(m // tm,),
        in_specs=[pl.BlockSpec((tm, d), lambda i: (i, 0)),
                  pl.BlockSpec((1, d), lambda i: (0, 0))],
        out_specs=pl.BlockSpec((tm, d), lambda i: (i, 0)),
        compiler_params=_params("parallel"),
        name="rms_cast",
    )(x, g.reshape(1, d))


ACT_ROWS = 2304


def _act_spec(tm, k):
    return pl.BlockSpec((tm, k), lambda i, j: (i, 0), pipeline_mode=pl.Buffered(1))


def _mm_kernel(a_ref, w_ref, o_ref):
    w = w_ref[...].astype(BF16)
    o_ref[...] = jnp.dot(a_ref[...], w, preferred_element_type=F32).astype(o_ref.dtype)


def matmul_cols(a, w, e, col0, ncols, out_dtype=F32):
    m, k = a.shape
    tn = _tile(ncols, 512, LANE)
    assert col0 % tn == 0
    cb = col0 // tn
    wide = ncols // tn >= 8
    tm = _tile(m, ACT_ROWS if wide else 1024, SUBLANE)
    return pl.pallas_call(
        _mm_kernel,
        out_shape=jax.ShapeDtypeStruct((m, ncols), out_dtype),
        grid=(m // tm, ncols // tn),
        in_specs=[_act_spec(tm, k) if wide else pl.BlockSpec((tm, k), lambda i, j: (i, 0)),
                  pl.BlockSpec((None, k, tn), lambda i, j: (e, 0, cb + j))],
        out_specs=pl.BlockSpec((tm, tn), lambda i, j: (i, j)),
        compiler_params=_params("parallel", "parallel"),
        name="matmul_cols",
    )(a, w)


def _swiglu_kernel(h_ref, wg_ref, wu_ref, o_ref):
    h = h_ref[...]
    g = jnp.dot(h, wg_ref[...].astype(BF16), preferred_element_type=F32)
    u = jnp.dot(h, wu_ref[...].astype(BF16), preferred_element_type=F32)
    o_ref[...] = (_silu(g) * u).astype(o_ref.dtype)


def swiglu_up(h, wg, wu, e):
    m, k = h.shape
    f = wg.shape[2]
    tm = _tile(m, ACT_ROWS, SUBLANE)
    tn = _tile(f, 256, LANE)
    wspec = pl.BlockSpec((None, k, tn), lambda i, j: (e, 0, j))
    return pl.pallas_call(
        _swiglu_kernel,
        out_shape=jax.ShapeDtypeStruct((m, f), BF16),
        grid=(m // tm, f // tn),
        in_specs=[_act_spec(tm, k), wspec, wspec],
        out_specs=pl.BlockSpec((tm, tn), lambda i, j: (i, j)),
        compiler_params=_params("parallel", "parallel"),
        name="swiglu_up",
    )(h, wg, wu)


EPILOGUE_ROWS = 2 * SUBLANE
EPILOGUE_UNROLL = 4


def _mm_resid_kernel(*refs, parts, n_main, has_tail, splits, n_chunk, with_next):
    a_refs, w_ref = refs[:len(parts)], refs[len(parts)]
    refs = refs[len(parts) + 1:]
    if has_tail:
        at_ref, wt_ref = refs[:2]
        refs = refs[2:]
    x_ref, gpost_ref, gnext_ref = refs[:3]
    xo_refs = refs[3:3 + len(splits)]
    ho_ref = refs[3 + len(splits)] if with_next else None
    inv_ref = refs[-1]
    i, k = pl.program_id(0), pl.program_id(1)
    tm, n = xo_refs[0].shape
    nk = n_main + int(has_tail)
    n_tiles = max(s[1] for s in splits)

    def in_range(idx, lo, hi, full):
        conds = ([idx >= lo] if lo > 0 else []) + ([idx < hi] if hi < full else [])
        return functools.reduce(lambda p, q: p & q, conds) if conds else None

    def when(*conds):
        conds = [c for c in conds if c is not None]
        return pl.when(functools.reduce(lambda p, q: p & q, conds)) if conds else (lambda f: f())

    def accumulate(a_blk, w_blk, xo_ref, first):
        a = a_blk[...]
        for c in range(0, n, n_chunk):
            part = jnp.dot(a, w_blk[:, c:c + n_chunk], preferred_element_type=F32)
            if first:
                xo_ref[:, c:c + n_chunk] = part
            else:
                xo_ref[:, c:c + n_chunk] += part

    def epilogue(xo_ref):
        lanes = [slice(j, j + LANE) for j in range(0, n, LANE)]

        def rows_of(c):
            if isinstance(c, int):
                return pl.ds(c * EPILOGUE_ROWS, EPILOGUE_ROWS)
            return pl.ds(pl.multiple_of(c * EPILOGUE_ROWS, EPILOGUE_ROWS), EPILOGUE_ROWS)

        def inv_rms(sq_lanes):
            ms = jnp.sum(sq_lanes, axis=-1, keepdims=True) * (1.0 / n)
            return jnp.broadcast_to(lax.rsqrt(ms + EPS), (EPILOGUE_ROWS, LANE))

        def stats_rows(c):
            rows = rows_of(c)
            sq = jnp.zeros((EPILOGUE_ROWS, LANE), F32)
            for ls in lanes:
                f = xo_ref[rows, ls]
                sq = sq + f * f
            return sq

        def resid_rows(c):
            rows = rows_of(c)
            r = inv_ref[rows, :]
            sq = jnp.zeros((EPILOGUE_ROWS, LANE), F32)
            for ls in lanes:
                xn = x_ref[rows, ls] + xo_ref[rows, ls] * r * gpost_ref[:, ls]
                xo_ref[rows, ls] = xn
                sq = sq + xn * xn
            return sq

        def next_step(c, carry):
            rows = rows_of(c)
            r = inv_ref[rows, :]
            for ls in lanes:
                ho_ref[rows, ls] = (xo_ref[rows, ls] * r * gnext_ref[:, ls]).astype(BF16)
            return carry

        steps = tm // EPILOGUE_ROWS
        u = EPILOGUE_UNROLL
        assert steps % u == 0

        def sweep(rows_fn, keep_stats):
            def batch(b):
                return tuple(rows_fn(b * u + j) for j in range(u))

            def finish(b, sqs):
                if keep_stats:
                    for j in range(u):
                        inv_ref[rows_of(b * u + j), :] = inv_rms(sqs[j])

            def step(b, sqs_prev):
                sqs = batch(b)
                finish(b - 1, sqs_prev)
                return sqs

            n_batches = steps // u
            finish(n_batches - 1, lax.fori_loop(1, n_batches, step, batch(0)))

        sweep(stats_rows, True)
        sweep(resid_rows, with_next)
        if with_next:
            lax.fori_loop(0, steps, next_step, 0, unroll=u)

    for (s0, s1), xo_ref in zip(splits, xo_refs):
        for (r0, r1, k0, k1), a_ref in zip(parts, a_refs):
            lo, hi = max(r0, s0), min(r1, s1)
            if lo >= hi:
                continue
            rows_ok = in_range(i, lo, hi, n_tiles)
            if k0 == 0:
                when(rows_ok, k == 0)(functools.partial(accumulate, a_ref, w_ref, xo_ref, True))
                if k1 > 1:
                    when(rows_ok, k > 0, in_range(k, 0, k1, nk))(
                        functools.partial(accumulate, a_ref, w_ref, xo_ref, False))
            else:
                when(rows_ok, in_range(k, k0, k1, nk))(functools.partial(accumulate, a_ref, w_ref, xo_ref, False))
        rows_ok = in_range(i, s0, s1, n_tiles)
        if has_tail:
            when(rows_ok, k == n_main)(functools.partial(accumulate, at_ref, wt_ref, xo_ref, False))
        when(rows_ok, k == nk - 1)(functools.partial(epilogue, xo_ref))


def matmul_resid_norm(a_parts, w, e, x, g_post, g_next, scale, out_rows=None):
    m, n = x.shape
    kdim = w.shape[1]
    edges = [row0 for _, row0, _ in a_parts] + list(itertools.accumulate(out_rows or []))
    tm = _tile(math.gcd(m, *edges), 512, EPILOGUE_ROWS)
    tk = min(512, kdim)
    n_main, tail = kdim // tk, kdim % tk
    has_tail = tail > 0
    nk = n_main + int(has_tail)
    with_next = g_next is not None
    assert not (with_next and out_rows is not None)
    gn = g_next if with_next else g_post
    gp = g_post if scale == 1.0 else g_post * scale
    row = lambda i, k: (i, 0)
    fixed = lambda i, k: (0, 0)

    def clipped(lo, hi):
        return lambda v: jnp.clip(v - lo, 0, hi - lo - 1)

    parts, in_specs, operands = [], [], []
    for arr, row0, col0 in a_parts:
        rows, cols = arr.shape
        assert row0 % tm == 0 and rows % tm == 0 and col0 % tk == 0
        assert cols % tk == 0 or (len(a_parts) == 1 and has_tail)
        part = (row0 // tm, (row0 + rows) // tm, col0 // tk, (col0 + cols) // tk)
        ri, ki = clipped(part[0], part[1]), clipped(part[2], part[3])
        parts.append(part)
        in_specs.append(pl.BlockSpec((tm, tk), lambda i, k, ri=ri, ki=ki: (ri(i), ki(k))))
        operands.append(arr)
    in_specs.append(pl.BlockSpec((None, tk, n), lambda i, k: (e, jnp.minimum(k, n_main - 1), 0)))
    operands.append(w)
    if has_tail:
        assert tail % LANE == 0 and (n_main * tk) % tail == 0
        tail_blk = (n_main * tk) // tail
        in_specs += [pl.BlockSpec((tm, tail), lambda i, k: (i, tail_blk)),
                     pl.BlockSpec((None, tail, n), lambda i, k: (e, tail_blk, 0))]
        operands += [a_parts[0][0], w]
    in_specs += [pl.BlockSpec((tm, n), row), pl.BlockSpec((1, n), fixed), pl.BlockSpec((1, n), fixed)]
    operands += [x, gp.reshape(1, n), gn.reshape(1, n)]

    splits, out_shape, out_specs, row0 = [], [], [], 0
    for rws in ([m] if out_rows is None else out_rows):
        split = (row0 // tm, (row0 + rws) // tm)
        ri = clipped(*split)
        splits.append(split)
        out_shape.append(jax.ShapeDtypeStruct((rws, n), F32))
        out_specs.append(pl.BlockSpec((tm, n), lambda i, k, ri=ri: (ri(i), 0)))
        row0 += rws
    if with_next:
        out_shape.append(jax.ShapeDtypeStruct((m, n), BF16))
        out_specs.append(pl.BlockSpec((tm, n), row))
    kern = functools.partial(_mm_resid_kernel, parts=tuple(parts), n_main=n_main, has_tail=has_tail,
                             splits=tuple(splits), n_chunk=_tile(n, 1024, LANE), with_next=with_next)
    outs = pl.pallas_call(
        kern,
        out_shape=out_shape,
        grid=(m // tm, nk),
        in_specs=in_specs,
        out_specs=out_specs,
        scratch_shapes=[pltpu.VMEM((tm, LANE), F32)],
        compiler_params=_params("arbitrary", "arbitrary",
                                vmem=VMEM_LIMIT if len(splits) == 1 else VMEM_LIMIT_SPLIT_OUT),
        name="matmul_resid_norm",
    )(*operands)
    if out_rows is not None:
        return tuple(outs), None
    return (outs[0], outs[1]) if with_next else (outs[0], None)


def _tap_rows(w):
    return jnp.repeat(w, SUBLANE, axis=0)


def _tap_weight(w_ref, j, cols, rows):
    tile = w_ref[j * SUBLANE:(j + 1) * SUBLANE, cols]
    return jnp.concatenate([tile] * (rows // SUBLANE), axis=0)


def _ab_block(go, gi, val, ga, gg, ca_halo, cb_halo, aw_ref, bw_ref, bias, lng, lnb,
              exta_ref, extb_ref, z_ref):
    r, c = go.shape
    ca = gi * val
    exta_ref[0:HALO_A, :] = ca_halo
    exta_ref[HALO_A:HALO_A + r, :] = ca
    conv_a = _tap_weight(aw_ref, CONV_A - 1, slice(None), r) * ca
    for j in range(CONV_A - 1):
        off = HALO_A - (CONV_A - 1) + j
        conv_a = conv_a + _tap_weight(aw_ref, j, slice(None), r) * exta_ref[off:off + r, :]
    y_a = go * conv_a
    cb = ga * jax.nn.sigmoid(gg)
    extb_ref[0:HALO_B, :] = cb_halo
    extb_ref[HALO_B:HALO_B + r, :] = cb
    rc = min(r, CONV_ROWS)
    cc = min(c, CONV_COLS)
    base = HALO_B - (CONV_B - 1)
    phases = {}
    for j in range(CONV_B):
        phases.setdefault((base + j) % SUBLANE, []).append(j)
    for r0 in range(0, r, rc):
        for c0 in range(0, c, cc):
            cols = slice(c0, c0 + cc)
            acc = jnp.broadcast_to(bias[:, cols], (rc, cc))
            for b, taps in sorted(phases.items()):
                span = SUBLANE * max((base + j) // SUBLANE for j in taps) + rc
                win = extb_ref[r0 + b:r0 + b + span, cols]
                for j in taps:
                    a = SUBLANE * ((base + j) // SUBLANE)
                    acc = acc + _tap_weight(bw_ref, j, cols, rc) * win[a:a + rc]
            z_ref[r0:r0 + rc, cols] = acc
    y_b = _silu(_layer_norm(z_ref[0:r, :], lng, lnb))
    return y_a, y_b, ca, cb


def _ab_prompt_kernel(go_ref, gi_ref, val_ref, ga_ref, gg_ref, gih_ref, valh_ref, gah_ref, ggh_ref,
                      aw_ref, bw_ref, bias_ref, lng_ref, lnb_ref,
                      y_ref, ca_tail_ref, cb_tail_ref, exta_ref, extb_ref, z_ref, *, n_i):
    i = pl.program_id(1)
    r, c = go_ref.shape
    first = i == 0
    ca_halo = jnp.where(first, 0.0, gih_ref[...] * valh_ref[...])
    cb_halo = jnp.where(first, 0.0, gah_ref[...] * jax.nn.sigmoid(ggh_ref[...]))
    y_a, y_b, ca, cb = _ab_block(go_ref[...], gi_ref[...], val_ref[...], ga_ref[...], gg_ref[...],
                                 ca_halo, cb_halo, aw_ref, bw_ref, bias_ref[...], lng_ref[...], lnb_ref[...],
                                 exta_ref, extb_ref, z_ref)
    y_ref[:, 0:c] = y_a.astype(BF16)
    y_ref[:, c:2 * c] = y_b.astype(BF16)

    @pl.when(i == n_i - 1)
    def _():
        ca_tail_ref[0] = exta_ref[r:r + HALO_A, :]
        cb_tail_ref[0] = extb_ref[r:r + HALO_B, :]


AB_SECTIONS = 5
CONV_ROWS, CONV_COLS = 64, 256


def mixer_ab_prompt(proj, n_b, t, a_w, b_w, b_bias, ln_g, ln_b):
    c = proj.shape[1] // AB_SECTIONS
    tr = _tile(t, 256, HALO_B)
    n_i = t // tr

    def cur(sec):
        return pl.BlockSpec((tr, c), lambda b, i: (b * n_i + i, sec))

    def halo(rows, sec):
        per = tr // rows
        return pl.BlockSpec((rows, c), lambda b, i: (jnp.maximum((b * n_i + i) * per - 1, 0), sec))

    def vec(rows):
        return pl.BlockSpec((rows, c), lambda b, i: (0, 0))

    kern = functools.partial(_ab_prompt_kernel, n_i=n_i)
    return pl.pallas_call(
        kern,
        out_shape=[jax.ShapeDtypeStruct((n_b * t, 2 * c), BF16),
                   jax.ShapeDtypeStruct((n_b, HALO_A, c), F32),
                   jax.ShapeDtypeStruct((n_b, HALO_B, c), F32)],
        grid=(n_b, n_i),
        in_specs=[cur(0), cur(1), cur(2), cur(3), cur(4),
                  halo(HALO_A, 1), halo(HALO_A, 2), halo(HALO_B, 3), halo(HALO_B, 4),
                  vec(CONV_A * SUBLANE), vec(CONV_B * SUBLANE), vec(1), vec(1), vec(1)],
        out_specs=[pl.BlockSpec((tr, 2 * c), lambda b, i: (b * n_i + i, 0)),
                   pl.BlockSpec((1, HALO_A, c), lambda b, i: (b, 0, 0)),
                   pl.BlockSpec((1, HALO_B, c), lambda b, i: (b, 0, 0))],
        scratch_shapes=[pltpu.VMEM((HALO_A + tr, c), F32), pltpu.VMEM((HALO_B + tr, c), F32),
                        pltpu.VMEM((tr, c), F32)],
        compiler_params=_params("parallel", "arbitrary"),
        name="mixer_ab_prompt",
    )(*([proj] * 9), _tap_rows(a_w), _tap_rows(b_w),
      b_bias.reshape(1, c), ln_g.reshape(1, c), ln_b.reshape(1, c))


def _ab_sample_kernel(go_ref, gi_ref, val_ref, ga_ref, gg_ref, ha_ref, hb_ref,
                      aw_ref, bw_ref, bias_ref, lng_ref, lnb_ref,
                      y_ref, ca_tail_ref, cb_tail_ref, exta_ref, extb_ref, z_ref, *, s):
    nb = ha_ref.shape[0]
    c = go_ref.shape[1]
    def one_stream(q, carry):
        rows = pl.ds(pl.multiple_of(q * s, s), s)
        y_a, y_b, ca, cb = _ab_block(go_ref[rows, :], gi_ref[rows, :], val_ref[rows, :], ga_ref[rows, :],
                                     gg_ref[rows, :], ha_ref[q], hb_ref[q], aw_ref, bw_ref, bias_ref[...],
                                     lng_ref[...], lnb_ref[...], exta_ref, extb_ref, z_ref)
        y_ref[rows, 0:c] = y_a.astype(BF16)
        y_ref[rows, c:2 * c] = y_b.astype(BF16)
        ca_tail_ref[q] = exta_ref[s:s + HALO_A, :]
        cb_tail_ref[q] = extb_ref[s:s + HALO_B, :]
        return carry

    lax.fori_loop(0, nb, one_stream, 0)


def mixer_ab_sample(proj, row0, n_b, s, hist_a, hist_b, a_w, b_w, b_bias, ln_g, ln_b):
    c = proj.shape[1] // AB_SECTIONS
    assert s >= HALO_B
    nb = _tile(n_b, 8, 1)
    assert row0 % (nb * s) == 0
    blk0 = row0 // (nb * s)
    ha = jnp.pad(hist_a, ((0, 0), (HALO_A - (CONV_A - 1), 0), (0, 0)))
    hb = jnp.pad(hist_b, ((0, 0), (HALO_B - (CONV_B - 1), 0), (0, 0)))
    def cur(sec):
        return pl.BlockSpec((nb * s, c), lambda i: (blk0 + i, sec))

    def vec(rows):
        return pl.BlockSpec((rows, c), lambda i: (0, 0))

    kern = functools.partial(_ab_sample_kernel, s=s)
    return pl.pallas_call(
        kern,
        out_shape=[jax.ShapeDtypeStruct((n_b * s, 2 * c), BF16),
                   jax.ShapeDtypeStruct((n_b, HALO_A, c), F32),
                   jax.ShapeDtypeStruct((n_b, HALO_B, c), F32)],
        grid=(n_b // nb,),
        in_specs=[cur(0), cur(1), cur(2), cur(3), cur(4),
                  pl.BlockSpec((nb, HALO_A, c), lambda i: (i, 0, 0)),
                  pl.BlockSpec((nb, HALO_B, c), lambda i: (i, 0, 0)),
                  vec(CONV_A * SUBLANE), vec(CONV_B * SUBLANE), vec(1), vec(1), vec(1)],
        out_specs=[pl.BlockSpec((nb * s, 2 * c), lambda i: (i, 0)),
                   pl.BlockSpec((nb, HALO_A, c), lambda i: (i, 0, 0)),
                   pl.BlockSpec((nb, HALO_B, c), lambda i: (i, 0, 0))],
        scratch_shapes=[pltpu.VMEM((HALO_A + s, c), F32), pltpu.VMEM((HALO_B + s, c), F32),
                        pltpu.VMEM((s, c), F32)],
        compiler_params=_params("parallel"),
        name="mixer_ab_sample",
    )(*([proj] * 5), ha, hb, _tap_rows(a_w), _tap_rows(b_w),
      b_bias.reshape(1, c), ln_g.reshape(1, c), ln_b.reshape(1, c))


def _rope_tables(pos):
    half = HEAD_DIM // 2
    inv_freq = ROPE_THETA ** (-jnp.arange(half, dtype=F32) / half)
    ang = pos.astype(F32)[:, None] * inv_freq[None, :]
    cos = jnp.cos(ang)
    sin = jnp.sin(ang)
    return jnp.concatenate([cos, cos], axis=-1), jnp.concatenate([-sin, sin], axis=-1)


def _rope(x, cos2, sin2):
    return x * cos2 + pltpu.roll(x, HEAD_DIM // 2, 1) * sin2


def _sink_softmax_pv(s, sink, vv):
    m = jnp.maximum(jnp.max(s, axis=-1, keepdims=True), sink)
    e = jnp.exp(s - m)
    p = e / (jnp.sum(e, axis=-1, keepdims=True) + jnp.exp(sink - m))
    return jnp.dot(p.astype(BF16), vv, preferred_element_type=F32)


def _qk(q, kk):
    return lax.dot_general(q, kk, (((1,), (1,)), ((), ())), preferred_element_type=F32) * (HEAD_DIM ** -0.5)


def _attn_prompt_kernel(sink_ref, q_ref, k_ref, v_ref, kh_ref, vh_ref, cos_ref, sin_ref, cosh_ref, sinh_ref,
                        att_ref, krot_ref):
    i = pl.program_id(1)
    tq = q_ref.shape[0]
    cos, sin = cos_ref[...], sin_ref[...]
    cosh, sinh = cosh_ref[...], sinh_ref[...]
    n_keys = WINDOW + tq
    q_chunk = lax.broadcasted_iota(jnp.int32, (tq, n_keys), 0) // CHUNK
    k_col = lax.broadcasted_iota(jnp.int32, (tq, n_keys), 1)
    k_chunk = k_col // CHUNK
    n_back = WINDOW // CHUNK
    mask = (k_chunk >= q_chunk) & (k_chunk <= q_chunk + n_back)
    mask = mask & (k_col >= jnp.where(i > 0, 0, WINDOW))
    for h in range(N_KV_HEADS):
        hs = slice(h * HEAD_DIM, (h + 1) * HEAD_DIM)
        k_cur = _rope(k_ref[:, hs], cos, sin)
        krot_ref[:, hs] = k_cur
        kk = jnp.concatenate([_rope(kh_ref[:, hs], cosh, sinh), k_cur], axis=0).astype(BF16)
        vv = jnp.concatenate([vh_ref[:, hs], v_ref[:, hs]], axis=0).astype(BF16)
        heads = [h * GQA_GROUP + g for g in range(GQA_GROUP)]
        cols = [slice(n * HEAD_DIM, (n + 1) * HEAD_DIM) for n in heads]
        qs = jnp.concatenate([_rope(q_ref[:, c], cos, sin) for c in cols], axis=0).astype(BF16)
        sink = jnp.concatenate([jnp.full((tq, 1), sink_ref[n], F32) for n in heads], axis=0)
        s = jnp.where(jnp.concatenate([mask] * GQA_GROUP, axis=0), _qk(qs, kk), -jnp.inf)
        o = _sink_softmax_pv(s, sink, vv).astype(BF16)
        for g, c in enumerate(cols):
            att_ref[:, c] = o[g * tq:(g + 1) * tq]


def attn_prompt(proj, qw, kw, sink, n_b, t):
    assert qw % kw == 0
    kb = qw // kw
    tq = _tile(t, 256, WINDOW)
    n_i = t // tq
    per = tq // WINDOW
    cos2, sin2 = _rope_tables(jnp.arange(t, dtype=jnp.int32))
    cur = lambda w, cb=0: pl.BlockSpec((tq, w), lambda b, i: (b * n_i + i, cb))
    halo = lambda cb: pl.BlockSpec((WINDOW, kw), lambda b, i: (jnp.maximum((b * n_i + i) * per - 1, 0), cb))
    tab = pl.BlockSpec((tq, HEAD_DIM), lambda b, i: (i, 0))
    tab_h = pl.BlockSpec((WINDOW, HEAD_DIM), lambda b, i: (jnp.maximum(i * per - 1, 0), 0))
    return pl.pallas_call(
        _attn_prompt_kernel,
        out_shape=[jax.ShapeDtypeStruct((n_b * t, qw), BF16), jax.ShapeDtypeStruct((n_b * t, kw), F32)],
        grid=(n_b, n_i),
        in_specs=[pl.BlockSpec(memory_space=pltpu.SMEM), cur(qw), cur(kw, kb), cur(kw, kb + 1),
                  halo(kb), halo(kb + 1), tab, tab, tab_h, tab_h],
        out_specs=[cur(qw), cur(kw)],
        compiler_params=_params("parallel", "parallel"),
        name="attn_prompt",
    )(sink.reshape(-1), proj, proj, proj, proj, proj, cos2, sin2, cos2, sin2)


def _attn_sample_kernel(sink_ref, q_ref, k_ref, v_ref, wk_ref, wv_ref, cos_ref, sin_ref, att_ref, krot_ref, *, s):
    nb = wk_ref.shape[0]
    cos, sin = cos_ref[...], sin_ref[...]
    def one_stream(b, carry):
        rows = pl.ds(pl.multiple_of(b * s, s), s)
        for h in range(N_KV_HEADS):
            hs = slice(h * HEAD_DIM, (h + 1) * HEAD_DIM)
            k_cur = _rope(k_ref[rows, hs], cos, sin)
            krot_ref[rows, hs] = k_cur
            kk = jnp.concatenate([wk_ref[b, :, hs], k_cur], axis=0).astype(BF16)
            vv = jnp.concatenate([wv_ref[b, :, hs], v_ref[rows, hs]], axis=0).astype(BF16)
            heads = [h * GQA_GROUP + g for g in range(GQA_GROUP)]
            cols = [slice(n * HEAD_DIM, (n + 1) * HEAD_DIM) for n in heads]
            qs = jnp.concatenate([_rope(q_ref[rows, c], cos, sin) for c in cols], axis=0).astype(BF16)
            sink = jnp.concatenate([jnp.full((s, 1), sink_ref[n], F32) for n in heads], axis=0)
            o = _sink_softmax_pv(_qk(qs, kk), sink, vv).astype(BF16)
            for g, c in enumerate(cols):
                att_ref[rows, c] = o[g * s:(g + 1) * s]
        return carry

    lax.fori_loop(0, nb, one_stream, 0)


def attn_sample(proj, qw, kw, row0, n_b, s, win_k, win_v, sink):
    assert qw % kw == 0
    kb = qw // kw
    nb = _tile(n_b, 8, 1)
    assert row0 % (nb * s) == 0
    blk0 = row0 // (nb * s)
    cos2, sin2 = _rope_tables(PAST_LEN + jnp.arange(s, dtype=jnp.int32))
    cur = lambda w, cb=0: pl.BlockSpec((nb * s, w), lambda i: (blk0 + i, cb))
    out = lambda w: pl.BlockSpec((nb * s, w), lambda i: (i, 0))
    win = pl.BlockSpec((nb, WINDOW, kw), lambda i: (i, 0, 0))
    tab = pl.BlockSpec((s, HEAD_DIM), lambda i: (0, 0))
    kern = functools.partial(_attn_sample_kernel, s=s)
    return pl.pallas_call(
        kern,
        out_shape=[jax.ShapeDtypeStruct((n_b * s, qw), BF16), jax.ShapeDtypeStruct((n_b * s, kw), F32)],
        grid=(n_b // nb,),
        in_specs=[pl.BlockSpec(memory_space=pltpu.SMEM), cur(qw), cur(kw, kb), cur(kw, kb + 1), win, win, tab, tab],
        out_specs=[out(qw), out(kw)],
        compiler_params=_params("parallel"),
        name="attn_sample",
    )(sink.reshape(-1), proj, proj, proj, win_k.reshape(n_b, WINDOW, kw), win_v.reshape(n_b, WINDOW, kw), cos2, sin2)


def _gmlp_kernel(u0_ref, u1_ref, vg0_ref, vg1_ref, lng_ref, lnb_ref, ws_ref, bst_ref, y_ref, *rest, with_vn):
    tr, half = u0_ref.shape
    c = 2 * half
    r = ws_ref.shape[1]
    gd = c // GMLP_GROUPS
    vn = _layer_norm(jnp.concatenate([vg0_ref[...], vg1_ref[...]], axis=1), lng_ref[...], lnb_ref[...])
    if with_vn:
        rest[0][...] = vn
    vnb = vn.astype(BF16)
    row_chunk = lax.broadcasted_iota(jnp.int32, (r, r), 0) // CHUNK
    col_chunk = lax.broadcasted_iota(jnp.int32, (r, r), 1) // CHUNK
    allowed = col_chunk <= row_chunk
    for g in range(GMLP_GROUPS):
        wg = jnp.where(allowed, ws_ref[g], 0.0).astype(BF16)
        bias = bst_ref[:, g:g + 1]
        cs = slice(g * gd, (g + 1) * gd)
        u_ref = u0_ref if g * gd < half else u1_ref
        us = slice((g * gd) % half, (g * gd) % half + gd)
        for r0 in range(0, tr, r):
            sg = jnp.dot(wg, vnb[r0:r0 + r, cs], preferred_element_type=F32) + bias
            y_ref[r0:r0 + r, cs] = (u_ref[r0:r0 + r, us] * sg).astype(BF16)


def gmlp(proj, col_u, col_vg, c, row0, n_rows, r, ln_g, ln_b, w_s, b_s, with_vn):
    half = c // 2
    assert col_u % half == 0 and col_vg % half == 0 and half % (c // GMLP_GROUPS) == 0
    tr = _tile(n_rows, 256, r)
    assert row0 % tr == 0
    blk0 = row0 // tr
    ws = w_s[:, :r, :r]
    bst = b_s[:, :r].T
    cur = lambda cb: pl.BlockSpec((tr, half), lambda i: (blk0 + i, cb))
    out = pl.BlockSpec((tr, c), lambda i: (i, 0))
    vec = pl.BlockSpec((1, c), lambda i: (0, 0))
    out_shape = [jax.ShapeDtypeStruct((n_rows, c), BF16)]
    out_specs = [out]
    if with_vn:
        out_shape.append(jax.ShapeDtypeStruct((n_rows, c), F32))
        out_specs.append(out)
    kern = functools.partial(_gmlp_kernel, with_vn=with_vn)
    outs = pl.pallas_call(
        kern,
        out_shape=out_shape,
        grid=(n_rows // tr,),
        in_specs=[cur(col_u // half), cur(col_u // half + 1), cur(col_vg // half), cur(col_vg // half + 1), vec, vec,
                  pl.BlockSpec((GMLP_GROUPS, r, r), lambda i: (0, 0, 0)),
                  pl.BlockSpec((r, GMLP_GROUPS), lambda i: (0, 0))],
        out_specs=out_specs,
        compiler_params=_params("parallel"),
        name="gmlp",
    )(proj, proj, proj, proj, ln_g.reshape(1, c), ln_b.reshape(1, c), ws, bst)
    return (outs[0], outs[1]) if with_vn else (outs[0], None)


def _xattn_kernel(q_ref, mk_ref, mv_ref, o_ref, *, tq):
    nb = mk_ref.shape[0]
    for b in range(nb):
        rows = slice(b * tq, (b + 1) * tq)
        for h in range(N_X_HEADS):
            hs = slice(h * HEAD_DIM, (h + 1) * HEAD_DIM)
            s = _qk(q_ref[rows, hs].astype(BF16), mk_ref[b, :, hs].astype(BF16))
            m = jnp.max(s, axis=-1, keepdims=True)
            e = jnp.exp(s - m)
            p = e / jnp.sum(e, axis=-1, keepdims=True)
            o = jnp.dot(p.astype(BF16), mv_ref[b, :, hs].astype(BF16), preferred_element_type=F32)
            o_ref[rows, hs] = o.astype(BF16)


def cross_attn(q, row0, n_b, t, mk, mv):
    w = q.shape[1]
    n_mem = mk.shape[1]
    if t >= 512:
        nb, tq = 1, _tile(t, 512, SUBLANE)
    else:
        nb, tq = _tile(n_b, 8, 1), t
    n_i = t // tq
    assert row0 % (nb * tq) == 0
    blk0 = row0 // (nb * tq)
    kern = functools.partial(_xattn_kernel, tq=tq)
    mem = pl.BlockSpec((nb, n_mem, w), lambda b, i: (b, 0, 0))
    return pl.pallas_call(
        kern,
        out_shape=jax.ShapeDtypeStruct((n_b * t, w), BF16),
        grid=(n_b // nb, n_i),
        in_specs=[pl.BlockSpec((nb * tq, w), lambda b, i: (blk0 + b * n_i + i, 0)), mem, mem],
        out_specs=pl.BlockSpec((nb * tq, w), lambda b, i: (b * n_i + i, 0)),
        compiler_params=_params("parallel", "parallel"),
        name="cross_attn",
    )(q, mk, mv)


def _stack3(w):
    return w.reshape((-1,) + w.shape[-2:])


def kernel(x_prompt, x_sample, mem_prompt, state_conv_a, state_conv_b, cache_win_k, cache_win_v, cache_mem_k, cache_mem_v, norms, ffn_w_gate, ffn_w_up, ffn_w_down, xattn_wq, xattn_wk, xattn_wv, xattn_wo, ab_w_in, a_conv_w, b_conv_w, b_conv_bias, b_ln_g, b_ln_b, ab_w_out, cd_w_in, c_sink, d_ln_g, d_ln_b, d_w_s, d_b_s, cd_w_out):
    n_bp, t, d = x_prompt.shape
    n_bs, s, _ = x_sample.shape
    depth = norms.shape[0]
    mp, ms = n_bp * t, n_bs * s
    n_mem = mem_prompt.shape[1]
    wx = xattn_wq.shape[2]

    x, h = rms_cast_join(x_prompt.reshape(mp, d), x_sample.reshape(ms, d), norms[0, 0])
    mem = mem_prompt.reshape(n_bp * n_mem, d)

    wg_all, wu_all = _stack3(ffn_w_gate), _stack3(ffn_w_up)
    wd_all = _stack3(ffn_w_down).astype(BF16)
    ab_out, cd_out, wo_all = ab_w_out.astype(BF16), cd_w_out.astype(BF16), xattn_wo.astype(BF16)

    def half_ffn(h, x, layer, half, g_post, g_next, out_rows=None):
        e = 2 * layer + half
        a = swiglu_up(h, wg_all, wu_all, e)
        return matmul_resid_norm([(a, 0, 0)], wd_all, e, x, g_post, g_next, 0.5, out_rows)

    conv_a_p, conv_a_s, conv_b_p, conv_b_s = [], [], [], []
    win_k_p, win_v_p, win_k_s, win_v_s, chunk_v_s = [], [], [], [], []
    mem_k_p, mem_v_p = [], []
    for layer in range(depth):
        g = norms[layer]
        x, h = half_ffn(h, x, layer, 0, g[1], g[2])

        if layer % 2 == 0:
            e = layer // 2
            proj = matmul_cols(h, ab_w_in, e, 0, ab_w_in.shape[2])
            conv = (a_conv_w[e], b_conv_w[e], b_conv_bias[e], b_ln_g[e], b_ln_b[e])
            y_p, ca_p, cb_p = mixer_ab_prompt(proj, n_bp, t, *conv)
            y_s, ca_s, cb_s = mixer_ab_sample(proj, mp, n_bs, s, state_conv_a[e], state_conv_b[e], *conv)
            conv_a_p.append(ca_p[:, HALO_A - (CONV_A - 1):])
            conv_a_s.append(ca_s[:, HALO_A - (CONV_A - 1):])
            conv_b_p.append(cb_p[:, HALO_B - (CONV_B - 1):])
            conv_b_s.append(cb_s[:, HALO_B - (CONV_B - 1):])
            y_parts = [(y_p, 0, 0), (y_s, mp, 0)]
            w_out, e_out = ab_out, e
        else:
            o = layer // 2
            wd_ = d_ln_g.shape[1]
            kvw = N_KV_HEADS * HEAD_DIM
            qw = cd_w_in.shape[2] - 2 * kvw - 2 * wd_
            proj = matmul_cols(h, cd_w_in, o, 0, cd_w_in.shape[2])
            col_v, col_u = qw + kvw, qw + 2 * kvw
            att_p, krot_p = attn_prompt(proj, qw, kvw, c_sink[o], n_bp, t)
            att_s, krot_s = attn_sample(proj, qw, kvw, mp, n_bs, s, cache_win_k[o], cache_win_v[o], c_sink[o])
            gm = (d_ln_g[o], d_ln_b[o], d_w_s[o], d_b_s[o])
            yd_p, _ = gmlp(proj, col_u, col_u + wd_, wd_, 0, mp, GMLP_CHUNK, *gm, False)
            yd_s, vn_s = gmlp(proj, col_u, col_u + wd_, wd_, mp, ms, s, *gm, True)
            kv_shape = (N_KV_HEADS, HEAD_DIM)
            win_k_p.append(krot_p.reshape(n_bp, t, *kv_shape)[:, -WINDOW:])
            v_tail = jnp.stack([proj[(b + 1) * t - WINDOW:(b + 1) * t, col_v:col_u] for b in range(n_bp)])
            win_v_p.append(v_tail.reshape(n_bp, WINDOW, *kv_shape))
            k_all = jnp.concatenate([cache_win_k[o], krot_s.reshape(n_bs, s, *kv_shape)], axis=1)
            v_all = jnp.concatenate([cache_win_v[o], proj[mp:, col_v:col_u].reshape(n_bs, s, *kv_shape)], axis=1)
            win_k_s.append(k_all[:, -WINDOW:])
            win_v_s.append(v_all[:, -WINDOW:])
            chunk_v_s.append(vn_s.reshape(n_bs, s, wd_))
            y_parts = [(att_p, 0, 0), (yd_p, 0, qw), (att_s, mp, 0), (yd_s, mp, qw)]
            w_out, e_out = cd_out, o
        x, h = matmul_resid_norm(y_parts, w_out, e_out, x, g[3], g[4], 1.0)

        m_n = rms_cast(mem, g[8])
        mk = matmul_cols(m_n, xattn_wk, layer, 0, wx)
        mv = matmul_cols(m_n, xattn_wv, layer, 0, wx)
        mem_k_p.append(mk.reshape(n_bp, n_mem, N_X_HEADS, HEAD_DIM))
        mem_v_p.append(mv.reshape(n_bp, n_mem, N_X_HEADS, HEAD_DIM))
        qx = matmul_cols(h, xattn_wq, layer, 0, wx)
        o_p = cross_attn(qx, 0, n_bp, t, mk.reshape(n_bp, n_mem, wx), mv.reshape(n_bp, n_mem, wx))
        o_s = cross_attn(qx, mp, n_bs, s, cache_mem_k[layer].reshape(n_bs, n_mem, wx),
                         cache_mem_v[layer].reshape(n_bs, n_mem, wx))
        x, h = matmul_resid_norm([(o_p, 0, 0), (o_s, mp, 0)], wo_all, layer, x, g[5], g[6], 1.0)

        if layer + 1 < depth:
            x, h = half_ffn(h, x, layer, 1, g[7], norms[layer + 1, 0])
        else:
            (xo_p, xo_s), _ = half_ffn(h, x, layer, 1, g[7], None, out_rows=[mp, ms])

    return (xo_p.reshape(n_bp, t, d), xo_s.reshape(n_bs, s, d),
            jnp.stack(conv_a_p), jnp.stack(conv_a_s), jnp.stack(conv_b_p), jnp.stack(conv_b_s),
            jnp.stack(win_k_p), jnp.stack(win_v_p), jnp.stack(win_k_s), jnp.stack(win_v_s),
            jnp.stack(chunk_v_s), jnp.stack(mem_k_p), jnp.stack(mem_v_p))
```
